```python
import math
import jax, jax.numpy as jnp
from jax import lax
import numpy as np

D_MODEL = 1024
BATCH = 8
SEQ = 8192
DEPTH = 4
DEC_BATCH = 32
DEC_SEQ = 64
PAST_LEN = 2048

CHUNK = 64
Q_BLOCK = 128
EPS = 1e-5
D_FF = 2816

SSD_WIDTH = 512
SSD_HEADS = 8
SSD_HEAD_DIM = 64
SSD_GROUPS = 2
SSD_STATE = 64
SSD_CONV = 4
SSD_CONV_CH = SSD_WIDTH + 2 * SSD_GROUPS * SSD_STATE

GLA_WIDTH = 256
GLA_HEADS = 4
GLA_DK = 32
GLA_DV = 64
GLA_LOWRANK = 16
GLA_TAU = 16.0

DIFF_WIDTH = 256
DIFF_HEADS = 4
DIFF_DK = 32
DIFF_DV = 64

MIX_WIDTH = SSD_WIDTH + GLA_WIDTH + DIFF_WIDTH
IN_SPLITS = (SSD_WIDTH, SSD_CONV_CH, SSD_HEADS,
             GLA_HEADS * GLA_DK, GLA_HEADS * GLA_DK, GLA_WIDTH, GLA_WIDTH, GLA_LOWRANK,
             DIFF_HEADS * 2 * DIFF_DK, DIFF_HEADS * 2 * DIFF_DK, DIFF_WIDTH)
IN_WIDTH = sum(IN_SPLITS)

kernel_name = 'hybrid_ssd_gla_diffattn_stream_step'


def _rmsnorm(x, g):
    x32 = x.astype(jnp.float32)
    y = x32 * lax.rsqrt(jnp.mean(x32 * x32, axis=-1, keepdims=True) + EPS)
    return (y * g.astype(jnp.float32)).astype(x.dtype)


def _swiglu(h, w_gate, w_up, w_down):
    return (jax.nn.silu(h @ w_gate) * (h @ w_up)) @ w_down


def _split_cols(t, sizes):
    out = []
    start = 0
    for s in sizes:
        out.append(t[..., start:start + s])
        start += s
    return out


def _chunks(t, c):
    b, L = t.shape[0], t.shape[1]
    return jnp.swapaxes(t.reshape(b, L // c, c, *t.shape[2:]), 0, 1)


def _unchunks(t):
    t = jnp.swapaxes(t, 0, 1)
    return t.reshape(t.shape[0], t.shape[1] * t.shape[2], *t.shape[3:])


def _ssd_scan(xd, a, bh, ch, h0):
    c = min(CHUNK, xd.shape[1])

    def step(h, inp):
        xd_c, a_c, b_c, c_c = inp
        n = xd_c.shape[1]
        cs = jnp.cumsum(a_c, axis=1)
        causal = jnp.tril(jnp.ones((n, n), bool))[None, :, :, None]
        decay = jnp.exp(jnp.where(causal, cs[:, :, None, :] - cs[:, None, :, :], -jnp.inf))
        scores = jnp.einsum('bihn,bjhn->bijh', c_c, b_c) * decay
        y = (jnp.einsum('bijh,bjhp->bihp', scores, xd_c)
             + jnp.einsum('bihn,bhpn->bihp', c_c * jnp.exp(cs)[..., None], h))
        h = (jnp.exp(cs[:, -1])[:, :, None, None] * h
             + jnp.einsum('bjhn,bjhp->bhpn', b_c * jnp.exp(cs[:, -1:] - cs)[..., None], xd_c))
        return h, y

    h, ys = lax.scan(step, h0, (_chunks(xd, c), _chunks(a, c), _chunks(bh, c), _chunks(ch, c)))
    return _unchunks(ys), h


def _gla_scan(q, k, v, g, s0):
    c = min(CHUNK, q.shape[1])

    def step(s, inp):
        q_c, k_c, v_c, g_c = inp
        n = q_c.shape[1]
        cs = jnp.cumsum(g_c, axis=1)
        causal = jnp.tril(jnp.ones((n, n), bool))[None, :, :, None, None]
        decay = jnp.exp(jnp.where(causal, cs[:, :, None] - cs[:, None], -jnp.inf))
        att = jnp.einsum('bihk,bjhk,bijhk->bijh', q_c, k_c, decay)
        o = (jnp.einsum('bijh,bjhv->bihv', att, v_c)
             + jnp.einsum('bihk,bhkv->bihv', q_c * jnp.exp(cs), s))
        s = (jnp.exp(cs[:, -1])[..., None] * s
             + jnp.einsum('bjhk,bjhv->bhkv', k_c * jnp.exp(cs[:, -1:] - cs), v_c))
        return s, o

    s, os_ = lax.scan(step, s0, (_chunks(q, c), _chunks(k, c), _chunks(v, c), _chunks(g, c)))
    return _unchunks(os_), s


def _diff_attend(q, k, v, q_pos, k_pos, lam):
    s = jnp.einsum('bqhmd,bkhmd->bhmqk', q.astype(jnp.float32), k.astype(jnp.float32)) * (DIFF_DK ** -0.5)
    mask = (k_pos // CHUNK)[None, :] <= (q_pos // CHUNK)[:, None]
    s = jnp.where(mask, s, -jnp.inf)
    p = jax.nn.softmax(s, axis=-1)
    w = p[:, :, 0] - lam * p[:, :, 1]
    return jnp.einsum('bhqk,bkhv->bqhv', w, v.astype(jnp.float32))


def _token_mix(h, conv_prev, ssd_h0, gla_s0, k_past, v_past, lam_init,
               w_in, w_out, conv_w, conv_b, dt_bias, a_log, d_skip, ssd_norm,
               gla_w_gate, gla_b_gate, gla_norm, lq1, lk1, lq2, lk2, diff_norm):
    f32 = jnp.float32
    b, L, _ = h.shape
    (z, xbc, dt_raw, g_q, g_k, g_v, g_r, g_a, a_q, a_k, a_v) = _split_cols(h @ w_in, IN_SPLITS)

    xpad = jnp.concatenate([conv_prev.astype(xbc.dtype), xbc], axis=1)
    conv = conv_b
    for i in range(SSD_CONV):
        conv = conv + xpad[:, i:i + L] * conv_w[i]
    conv = jax.nn.silu(conv)
    new_conv = xpad[:, xpad.shape[1] - (SSD_CONV - 1):]
    xs, bm, cm = _split_cols(conv, (SSD_WIDTH, SSD_GROUPS * SSD_STATE, SSD_GROUPS * SSD_STATE))
    dt = jax.nn.softplus(dt_raw.astype(f32) + dt_bias.astype(f32))
    a = dt * (-jnp.exp(a_log.astype(f32)))
    xh = xs.astype(f32).reshape(b, L, SSD_HEADS, SSD_HEAD_DIM)
    rep = SSD_HEADS // SSD_GROUPS
    bh = jnp.repeat(bm.astype(f32).reshape(b, L, SSD_GROUPS, SSD_STATE), rep, axis=2)
    chh = jnp.repeat(cm.astype(f32).reshape(b, L, SSD_GROUPS, SSD_STATE), rep, axis=2)
    y, ssd_h = _ssd_scan(xh * dt[..., None], a, bh, chh, ssd_h0.astype(f32))
    y = y + d_skip.astype(f32)[:, None] * xh
    y_ssd = _rmsnorm(y.reshape(b, L, SSD_WIDTH) * jax.nn.silu(z.astype(f32)), ssd_norm).astype(h.dtype)

    q = g_q.astype(f32).reshape(b, L, GLA_HEADS, GLA_DK) * (GLA_DK ** -0.5)
    k = g_k.astype(f32).reshape(b, L, GLA_HEADS, GLA_DK)
    v = g_v.astype(f32).reshape(b, L, GLA_HEADS, GLA_DV)
    g = jax.nn.log_sigmoid((g_a @ gla_w_gate + gla_b_gate).astype(f32)) / GLA_TAU
    g = g.reshape(b, L, GLA_HEADS, GLA_DK)
    o, gla_s = _gla_scan(q, k, v, g, gla_s0.astype(f32))
    y_gla = (_rmsnorm(o, gla_norm).reshape(b, L, GLA_WIDTH) * jax.nn.silu(g_r.astype(f32))).astype(h.dtype)

    lam = (jnp.exp(jnp.sum(lq1.astype(f32) * lk1.astype(f32)))
           - jnp.exp(jnp.sum(lq2.astype(f32) * lk2.astype(f32))) + lam_init)
    q_d = a_q.reshape(b, L, DIFF_HEADS, 2, DIFF_DK)
    k_d = a_k.reshape(b, L, DIFF_HEADS, 2, DIFF_DK)
    v_d = a_v.reshape(b, L, DIFF_HEADS, DIFF_DV)
    if k_past is None:
        k_pos = jnp.arange(L)

        def q_block(args):
            qb, i = args
            return _diff_attend(qb, k_d, v_d, i * Q_BLOCK + jnp.arange(Q_BLOCK), k_pos, lam)

        o_d = _unchunks(lax.map(q_block, (_chunks(q_d, Q_BLOCK), jnp.arange(L // Q_BLOCK))))
    else:
        p_len = k_past.shape[1]
        k_all = jnp.concatenate(
            [k_past.astype(k_d.dtype).reshape(b, p_len, DIFF_HEADS, 2, DIFF_DK), k_d], axis=1)
        v_all = jnp.concatenate([v_past.astype(v_d.dtype), v_d], axis=1)
        o_d = _diff_attend(q_d, k_all, v_all, p_len + jnp.arange(L), jnp.arange(p_len + L), lam)
    y_diff = (_rmsnorm(o_d, diff_norm) * (1.0 - lam_init)).reshape(b, L, DIFF_WIDTH).astype(h.dtype)

    out = jnp.concatenate([y_ssd, y_gla, y_diff], axis=-1) @ w_out
    return (out, new_conv, ssd_h.astype(h.dtype), gla_s.astype(h.dtype),
            k_d.reshape(b, L, DIFF_HEADS, 2 * DIFF_DK), v_d)


def _trunk(x, conv_st, ssd_st, gla_st, k_cache, v_cache, W):
    b = x.shape[0]
    prompt = k_cache is None
    convs, ssds, glas, ks, vs = [], [], [], [], []
    for l in range(DEPTH):
        lam_init = 0.8 - 0.6 * math.exp(-0.3 * l)
        x = x + 0.5 * _swiglu(_rmsnorm(x, W['ffn1_norm'][l]), W['ffn1_w_gate'][l],
                              W['ffn1_w_up'][l], W['ffn1_w_down'][l])
        if prompt:
            conv_prev = jnp.zeros((b, SSD_CONV - 1, SSD_CONV_CH), x.dtype)
            ssd_h0 = jnp.zeros((b, SSD_HEADS, SSD_HEAD_DIM, SSD_STATE), jnp.float32)
            gla_s0 = jnp.zeros((b, GLA_HEADS, GLA_DK, GLA_DV), jnp.float32)
            k_past, v_past = None, None
        else:
            conv_prev, ssd_h0, gla_s0 = conv_st[l], ssd_st[l], gla_st[l]
            k_past, v_past = k_cache[l], v_cache[l]
        m, c_new, s_new, g_new, k_new, v_new = _token_mix(
            _rmsnorm(x, W['mix_norm'][l]), conv_prev, ssd_h0, gla_s0, k_past, v_past, lam_init,
            W['w_in'][l], W['w_out'][l], W['ssd_conv_w'][l], W['ssd_conv_b'][l], W['ssd_dt_bias'][l],
            W['ssd_a_log'][l], W['ssd_d'][l], W['ssd_norm'][l], W['gla_w_gate'][l], W['gla_b_gate'][l],
            W['gla_norm'][l], W['diff_lambda_q1'][l], W['diff_lambda_k1'][l], W['diff_lambda_q2'][l],
            W['diff_lambda_k2'][l], W['diff_norm'][l])
        x = x + m
        x = x + 0.5 * _swiglu(_rmsnorm(x, W['ffn2_norm'][l]), W['ffn2_w_gate'][l],
                              W['ffn2_w_up'][l], W['ffn2_w_down'][l])
        convs.append(c_new)
        ssds.append(s_new)
        glas.append(g_new)
        ks.append(k_new)
        vs.append(v_new)
    return (_rmsnorm(x, W['final_norm']), jnp.stack(convs), jnp.stack(ssds), jnp.stack(glas),
            jnp.stack(ks), jnp.stack(vs))


def setup_inputs(seed: int = 0) -> dict:
    key = jax.random.key(seed)
    ks = iter(jax.random.split(key, 48))

    def nrm(shape, scale):
        return scale * jax.random.normal(next(ks), shape, jnp.float32)

    def gain(shape):
        return 1.0 + 0.01 * jax.random.normal(next(ks), shape, jnp.float32)

    x_prompt = nrm((BATCH, SEQ, D_MODEL), 1.0)
    x_sample = nrm((DEC_BATCH, DEC_SEQ, D_MODEL), 1.0)
    state_ssd_conv = nrm((DEPTH, DEC_BATCH, SSD_CONV - 1, SSD_CONV_CH), 1.0)
    state_ssd = nrm((DEPTH, DEC_BATCH, SSD_HEADS, SSD_HEAD_DIM, SSD_STATE), 0.1)
    state_gla = nrm((DEPTH, DEC_BATCH, GLA_HEADS, GLA_DK, GLA_DV), 0.1)
    cache_diff_k = nrm((DEPTH, DEC_BATCH, PAST_LEN, DIFF_HEADS, 2 * DIFF_DK), 1.0)
    cache_diff_v = nrm((DEPTH, DEC_BATCH, PAST_LEN, DIFF_HEADS, DIFF_DV), 1.0)

    u = jax.random.uniform(next(ks), (DEPTH, SSD_HEADS), jnp.float32)
    dt0 = jnp.exp(u * (math.log(0.1) - math.log(0.001)) + math.log(0.001))
    ssd_dt_bias = dt0 + jnp.log(-jnp.expm1(-dt0))
    ssd_a_log = jnp.log(jax.random.uniform(next(ks), (DEPTH, SSD_HEADS), jnp.float32, 1.0, 16.0))

    return {
        'x_prompt': x_prompt,
        'x_sample': x_sample,
        'state_ssd_conv': state_ssd_conv,
        'state_ssd': state_ssd,
        'state_gla': state_gla,
        'cache_diff_k': cache_diff_k,
        'cache_diff_v': cache_diff_v,
        'ffn1_norm': gain((DEPTH, D_MODEL)),
        'ffn1_w_gate': nrm((DEPTH, D_MODEL, D_FF), D_MODEL ** -0.5),
        'ffn1_w_up': nrm((DEPTH, D_MODEL, D_FF), D_MODEL ** -0.5),
        'ffn1_w_down': nrm((DEPTH, D_FF, D_MODEL), D_FF ** -0.5),
        'mix_norm': gain((DEPTH, D_MODEL)),
        'w_in': nrm((DEPTH, D_MODEL, IN_WIDTH), D_MODEL ** -0.5),
        'w_out': nrm((DEPTH, MIX_WIDTH, D_MODEL), MIX_WIDTH ** -0.5),
        'ssd_conv_w': nrm((DEPTH, SSD_CONV, SSD_CONV_CH), SSD_CONV ** -0.5),
        'ssd_conv_b': nrm((DEPTH, SSD_CONV_CH), 0.01),
        'ssd_dt_bias': ssd_dt_bias,
        'ssd_a_log': ssd_a_log,
        'ssd_d': gain((DEPTH, SSD_HEADS)),
        'ssd_norm': gain((DEPTH, SSD_WIDTH)),
        'gla_w_gate': nrm((DEPTH, GLA_LOWRANK, GLA_HEADS * GLA_DK), GLA_LOWRANK ** -0.5),
        'gla_b_gate': nrm((DEPTH, GLA_HEADS * GLA_DK), 0.01),
        'gla_norm': gain((DEPTH, GLA_DV)),
        'diff_lambda_q1': nrm((DEPTH, DIFF_DK), 0.1),
        'diff_lambda_k1': nrm((DEPTH, DIFF_DK), 0.1),
        'diff_lambda_q2': nrm((DEPTH, DIFF_DK), 0.1),
        'diff_lambda_k2': nrm((DEPTH, DIFF_DK), 0.1),
        'diff_norm': gain((DEPTH, DIFF_DV)),
        'ffn2_norm': gain((DEPTH, D_MODEL)),
        'ffn2_w_gate': nrm((DEPTH, D_MODEL, D_FF), D_MODEL ** -0.5),
        'ffn2_w_up': nrm((DEPTH, D_MODEL, D_FF), D_MODEL ** -0.5),
        'ffn2_w_down': nrm((DEPTH, D_FF, D_MODEL), D_FF ** -0.5),
        'final_norm': gain((D_MODEL,)),
    }


def reference(x_prompt, x_sample, state_ssd_conv, state_ssd, state_gla, cache_diff_k, cache_diff_v,
              ffn1_norm, ffn1_w_gate, ffn1_w_up, ffn1_w_down, mix_norm, w_in, w_out,
              ssd_conv_w, ssd_conv_b, ssd_dt_bias, ssd_a_log, ssd_d, ssd_norm,
              gla_w_gate, gla_b_gate, gla_norm,
              diff_lambda_q1, diff_lambda_k1, diff_lambda_q2, diff_lambda_k2, diff_norm,
              ffn2_norm, ffn2_w_gate, ffn2_w_up, ffn2_w_down, final_norm):
    W = dict(ffn1_norm=ffn1_norm, ffn1_w_gate=ffn1_w_gate, ffn1_w_up=ffn1_w_up, ffn1_w_down=ffn1_w_down,
             mix_norm=mix_norm, w_in=w_in, w_out=w_out,
             ssd_conv_w=ssd_conv_w, ssd_conv_b=ssd_conv_b, ssd_dt_bias=ssd_dt_bias,
             ssd_a_log=ssd_a_log, ssd_d=ssd_d, ssd_norm=ssd_norm,
             gla_w_gate=gla_w_gate, gla_b_gate=gla_b_gate, gla_norm=gla_norm,
             diff_lambda_q1=diff_lambda_q1, diff_lambda_k1=diff_lambda_k1,
             diff_lambda_q2=diff_lambda_q2, diff_lambda_k2=diff_lambda_k2, diff_norm=diff_norm,
             ffn2_norm=ffn2_norm, ffn2_w_gate=ffn2_w_gate, ffn2_w_up=ffn2_w_up, ffn2_w_down=ffn2_w_down,
             final_norm=final_norm)
    y_prompt, conv_p, ssd_p, gla_p, k_p, v_p = _trunk(x_prompt, None, None, None, None, None, W)
    y_sample, conv_s, ssd_s, gla_s, k_s, v_s = _trunk(
        x_sample, state_ssd_conv, state_ssd, state_gla, cache_diff_k, cache_diff_v, W)
    return (y_prompt, y_sample, conv_p, conv_s, ssd_p, ssd_s, gla_p, gla_s, k_p, k_s, v_p, v_s)
```

```python
import functools

import jax
import jax.numpy as jnp
import numpy as np
from jax import lax
from jax.experimental import pallas as pl
from jax.experimental.pallas import tpu as pltpu

F32 = jnp.float32
BF16 = jnp.bfloat16

EPS = 1e-5
DEPTH = 4
D_MODEL = 1024
D_FF = 2816
CHUNK = 64

SSD_WIDTH = 512
SSD_HEADS = 8
SSD_HEAD_DIM = 64
SSD_GROUPS = 2
SSD_STATE = 64
SSD_CONV = 4
SSD_CONV_CH = SSD_WIDTH + 2 * SSD_GROUPS * SSD_STATE

GLA_WIDTH = 256
GLA_HEADS = 4
GLA_DK = 32
GLA_DV = 64
GLA_LOWRANK = 16
GLA_TAU = 16.0

DIFF_WIDTH = 256
DIFF_HEADS = 4
DIFF_DK = 32
DIFF_DV = 64

IN_SPLITS = (SSD_WIDTH, SSD_CONV_CH, SSD_HEADS,
             GLA_HEADS * GLA_DK, GLA_HEADS * GLA_DK, GLA_WIDTH, GLA_WIDTH, GLA_LOWRANK,
             DIFF_HEADS * 2 * DIFF_DK, DIFF_HEADS * 2 * DIFF_DK, DIFF_WIDTH)

V7X_LANES = 128
V7X_SUBLANES = 8
V7X_VMEM_BYTES = 64 * 1024 * 1024

_PAD_SMALL = V7X_LANES - SSD_HEADS - GLA_LOWRANK
IN_PADDED = sum(IN_SPLITS) + _PAD_SMALL
O_Z, O_XBC, O_GQ, O_GV, O_GR, O_AQ, O_AK, O_AV, O_DTGA = 0, 512, 1280, 1536, 1792, 2048, 2304, 2560, 2816

TM = 512
SEQ_TILE = 512
SCAN_CHUNK = 128
ATTN_TILE = 512

_NT = (((1,), (1,)), ((), ()))
_TN = (((0,), (0,)), ((), ()))


def _vmem_limit(nbytes):
    return int(min(max(nbytes, 16 * 1024 * 1024), V7X_VMEM_BYTES - 6 * 1024 * 1024))


def _rmsnorm(x, g):
    ms = jnp.mean(x * x, axis=-1, keepdims=True)
    return x * lax.rsqrt(ms + EPS) * g


def _silu(x):
    return x * jax.nn.sigmoid(x)


def _split3(x):
    hi = x.astype(BF16)
    r = x - hi.astype(F32)
    mid = r.astype(BF16)
    lo = (r - mid.astype(F32)).astype(BF16)
    return hi, mid, lo


def _dot_exact_l(a, x):
    return sum(jnp.dot(a, p, preferred_element_type=F32) for p in _split3(x))


def _dot_exact_r(x, a):
    return sum(jnp.dot(p, a, preferred_element_type=F32) for p in _split3(x))


def _dot_exact_nt(a, x):
    return sum(lax.dot_general(a, p, _NT, preferred_element_type=F32) for p in _split3(x))


def _resident(block_shape, index_map):
    return pl.BlockSpec(block_shape, index_map, pipeline_mode=pl.Buffered(1))


FF_SPLIT = 2


def _ffn_kernel(*refs, with_mix, with_final):
    it = iter(refs)
    x_ref = next(it)
    if with_mix:
        y1_ref, y2_ref, y3_ref, wo_ref = next(it), next(it), next(it), next(it)
    g_ref, wg_ref, wu_ref, wd_ref = next(it), next(it), next(it), next(it)
    if with_final:
        fn_ref = next(it)
    o_ref = next(it)

    x = x_ref[...]
    if with_mix:
        y = jnp.concatenate([y1_ref[...], y2_ref[...], y3_ref[...]], axis=-1)
        x = x + jnp.dot(y, wo_ref[...], preferred_element_type=F32)
    h = _rmsnorm(x, g_ref[...]).astype(BF16)
    fb = D_FF // FF_SPLIT
    acc = None
    for c in range(FF_SPLIT):
        gate = jnp.dot(h, wg_ref[:, c * fb:(c + 1) * fb], preferred_element_type=F32)
        up = jnp.dot(h, wu_ref[:, c * fb:(c + 1) * fb], preferred_element_type=F32)
        act = (_silu(gate) * up).astype(BF16)
        part = jnp.dot(act, wd_ref[c * fb:(c + 1) * fb, :], preferred_element_type=F32)
        acc = part if acc is None else acc + part
    x = x + 0.5 * acc
    if with_final:
        x = _rmsnorm(x, fn_ref[...])
    o_ref[...] = x


def _ffn_call(x, layer, norm, wg, wu, wd, mix=None, final_norm=None):
    t, d = x.shape
    tm = min(TM, t)
    tile = lambda w: pl.BlockSpec((tm, w), lambda i: (i, 0))
    per_layer = lambda shape: _resident((None,) + shape, lambda i: (layer, 0, 0))
    args, specs = [x], [tile(d)]
    if mix is not None:
        y1, y2, y3, wo = mix
        args += [y1, y2, y3, wo]
        specs += [tile(y1.shape[1]), tile(y2.shape[1]), tile(y3.shape[1]), per_layer((d, d))]
    args += [norm, wg, wu, wd]
    specs += [per_layer((1, d)), per_layer((d, D_FF)), per_layer((d, D_FF)), per_layer((D_FF, d))]
    if final_norm is not None:
        args.append(final_norm)
        specs.append(_resident((1, d), lambda i: (0, 0)))
    weights = 2 * (3 * d * D_FF + d * d)
    tiles = 2 * (2 * tm * d * 4 + tm * d * 2)
    temps = tm * (3 * d * 4 + (D_FF // FF_SPLIT) * (4 + 4 + 2))
    return pl.pallas_call(
        functools.partial(_ffn_kernel, with_mix=mix is not None, with_final=final_norm is not None),
        grid=(t // tm,),
        in_specs=specs,
        out_specs=tile(d),
        out_shape=jax.ShapeDtypeStruct((t, d), F32),
        compiler_params=pltpu.CompilerParams(
            dimension_semantics=("arbitrary",), vmem_limit_bytes=_vmem_limit(weights + tiles + 2 * temps)),
        name="ffn",
    )(*args)


def _inproj_kernel(x_ref, g_ref, w_ref, z_ref, xbc_ref, gqk_ref, gv_ref, gr_ref, aq_ref, ak_ref, av_ref, dtga_ref):
    h = _rmsnorm(x_ref[...], g_ref[...]).astype(BF16)
    u = jnp.dot(h, w_ref[...], preferred_element_type=F32)
    z_ref[...] = u[:, O_Z:O_XBC].astype(BF16)
    xbc_ref[...] = u[:, O_XBC:O_GQ]
    gqk_ref[...] = u[:, O_GQ:O_GV]
    gv_ref[...] = u[:, O_GV:O_GR].astype(BF16)
    gr_ref[...] = u[:, O_GR:O_AQ].astype(BF16)
    aq_ref[...] = (u[:, O_AQ:O_AK] * (DIFF_DK ** -0.5 * np.log2(np.e))).astype(BF16)
    ak_ref[...] = u[:, O_AK:O_AV]
    av_ref[...] = u[:, O_AV:O_DTGA]
    dtga_ref[...] = u[:, O_DTGA:IN_PADDED]


def _inproj_call(x, layer, norm, w):
    t, d = x.shape
    widths = [(O_XBC - O_Z, BF16), (O_GQ - O_XBC, F32), (O_GV - O_GQ, F32), (O_GR - O_GV, BF16),
              (O_AQ - O_GR, BF16), (O_AK - O_AQ, BF16), (O_AV - O_AK, F32), (O_DTGA - O_AV, F32),
              (IN_PADDED - O_DTGA, F32)]
    tm = min(TM, t)
    tile = lambda w_: pl.BlockSpec((tm, w_), lambda i: (i, 0))
    per_layer = lambda shape: _resident((None,) + shape, lambda i: (layer, 0, 0))
    est = 2 * d * IN_PADDED + 2 * tm * d * 4 + 3 * tm * IN_PADDED * 4
    return pl.pallas_call(
        _inproj_kernel,
        grid=(t // tm,),
        in_specs=[tile(d), per_layer((1, d)), per_layer((d, IN_PADDED))],
        out_specs=[tile(w_) for w_, _ in widths],
        out_shape=[jax.ShapeDtypeStruct((t, w_), dt) for w_, dt in widths],
        compiler_params=pltpu.CompilerParams(
            dimension_semantics=("arbitrary",), vmem_limit_bytes=_vmem_limit(est)),
        name="inproj",
    )(x, norm, w)


def _ssd_kernel(xbc_ref, z_ref, dtga_ref, cprev_ref, s0_ref, cw_ref, cb_ref, dtb_ref, alog_ref, dexp_ref,
                norm_ref, ltri_ref, ident_ref, expand_ref,
                y_ref, cout_ref, sout_ref,
                xp_scr, conv_scr, a_scr, dt_scr, s_scr, *, tt, cs):
    t = pl.program_id(1)
    hdr = V7X_SUBLANES

    @pl.when(t == 0)
    def _():
        xp_scr[0:hdr, :] = cprev_ref[...]
        s_scr[...] = s0_ref[...]

    xp_scr[hdr:hdr + tt, :] = xbc_ref[...]
    conv = cb_ref[...]
    for i in range(SSD_CONV):
        off = hdr - (SSD_CONV - 1) + i
        conv = conv + xp_scr[off:off + tt, :] * cw_ref[i:i + 1, :]
    conv_scr[...] = _silu(conv)
    xp_scr[0:hdr, :] = xp_scr[tt:tt + hdr, :]

    lane = lax.broadcasted_iota(jnp.int32, (tt, V7X_LANES), 1)
    x_dt = dtga_ref[...] + dtb_ref[...]
    dt = jnp.maximum(x_dt, 0.0) + jnp.log1p(jnp.exp(-jnp.abs(x_dt)))
    dt = jnp.where(lane < SSD_HEADS, dt, 0.0)
    dt_scr[...] = dt
    a_scr[...] = dt * (-jnp.exp(alog_ref[...]))

    left = lax.broadcasted_iota(jnp.int32, (cs, V7X_LANES), 1) < SSD_STATE
    ri = lax.broadcasted_iota(jnp.int32, (cs, cs), 0)
    ci = lax.broadcasted_iota(jnp.int32, (cs, cs), 1)
    causal = ci <= ri
    bi = lax.broadcasted_iota(jnp.int32, (V7X_LANES, V7X_LANES), 0) // SSD_STATE
    bj = lax.broadcasted_iota(jnp.int32, (V7X_LANES, V7X_LANES), 1) // SSD_HEAD_DIM
    blockdiag = bi == bj

    def chunk(c, carry):
        r0 = pl.multiple_of(c * cs, cs)
        conv_c = conv_scr[pl.ds(r0, cs), :]
        xs = conv_c[:, :SSD_WIDTH]
        bm = conv_c[:, SSD_WIDTH:SSD_WIDTH + V7X_LANES]
        cm = conv_c[:, SSD_WIDTH + V7X_LANES:]
        a_c = a_scr[pl.ds(r0, cs), :]
        dt_c = dt_scr[pl.ds(r0, cs), :]
        cum = _dot_exact_l(ltri_ref[...], a_c)
        cum_t = _dot_exact_nt(ident_ref[...], cum)
        last = cum[cs - 1:cs, :]
        e_cum = jnp.exp(cum)
        e_rev = jnp.exp(last - cum)
        small = jnp.concatenate([dt_c, e_cum, e_rev], axis=0).astype(BF16)
        wide = jnp.dot(small, expand_ref[...], preferred_element_type=F32)
        dt_x, ecum_x, erev_x = wide[0:cs], wide[cs:2 * cs], wide[2 * cs:3 * cs]
        elast_x = _dot_exact_r(jnp.broadcast_to(e_cum[cs - 1:cs, :], (V7X_SUBLANES, V7X_LANES)),
                               expand_ref[...])[0:1, :]
        xd = xs * dt_x
        bm_sw = pltpu.roll(bm, SSD_STATE, axis=1)
        cm_sw = pltpu.roll(cm, SSD_STATE, axis=1)
        ys = []
        for g in range(SSD_GROUPS):
            if g == 0:
                bdup, cdup, cmask = jnp.where(left, bm, bm_sw), jnp.where(left, cm, cm_sw), jnp.where(left, cm, 0.0)
            else:
                bdup, cdup, cmask = jnp.where(left, bm_sw, bm), jnp.where(left, cm_sw, cm), jnp.where(left, 0.0, cm)
            gmat = lax.dot_general(cmask.astype(BF16), bm.astype(BF16), _NT, preferred_element_type=F32)
            for pr in (2 * g, 2 * g + 1):
                sl = slice(V7X_LANES * pr, V7X_LANES * (pr + 1))
                xd_p = xd[:, sl]
                top = jnp.where(left, xd_p, 0.0).astype(BF16)
                bot = jnp.where(left, 0.0, xd_p).astype(BF16)
                scores = []
                for hd in range(2):
                    h = 2 * pr + hd
                    arg = cum[:, h:h + 1] - cum_t[h:h + 1, :]
                    decay = jnp.where(causal, jnp.exp(jnp.minimum(arg, 0.0)), 0.0)
                    scores.append((gmat * decay).astype(BF16))
                ce = (cdup * ecum_x[:, sl]).astype(BF16)
                s_pair = s_scr[pr]
                lhs = jnp.concatenate([scores[0], scores[1], ce], axis=1)
                rhs = jnp.concatenate([top, bot, s_pair.astype(BF16)], axis=0)
                ys.append(jnp.dot(lhs, rhs, preferred_element_type=F32))
                bt = (bdup * erev_x[:, sl]).astype(BF16)
                upd = lax.dot_general(bt, xd_p.astype(BF16), _TN, preferred_element_type=F32)
                s_scr[pr] = s_pair * elast_x[:, sl] + jnp.where(blockdiag, upd, 0.0)
        y = jnp.concatenate(ys, axis=1) + dexp_ref[...] * xs
        y = y * _silu(z_ref[pl.ds(r0, cs), :].astype(F32))
        y_ref[pl.ds(r0, cs), :] = _rmsnorm(y, norm_ref[...]).astype(BF16)
        return carry

    lax.fori_loop(0, tt // cs, chunk, 0)

    @pl.when(t == pl.num_programs(1) - 1)
    def _():
        cout_ref[...] = xp_scr[0:hdr, :]
        sout_ref[...] = s_scr[...]


def _ssd_call(xbc, z, dtga, conv_prev, s0, consts, layer, batch, seq):
    tt = min(SEQ_TILE, seq)
    cs = min(SCAN_CHUNK, seq)
    nt = seq // tt
    hdr = V7X_SUBLANES
    tile = lambda w: pl.BlockSpec((tt, w), lambda b, t: (b * nt + t, 0))
    per_layer = lambda shape: _resident((None,) + shape, lambda b, t: (layer, 0, 0))
    const = lambda a: _resident(a.shape, lambda b, t: (0,) * a.ndim)
    npair = SSD_HEADS // 2
    ltri = jnp.asarray(np.tril(np.ones((cs, cs), np.float32)), BF16)
    ident = jnp.asarray(np.eye(V7X_LANES, dtype=np.float32), BF16)
    ex = np.zeros((V7X_LANES, SSD_WIDTH), np.float32)
    for h in range(SSD_HEADS):
        ex[h, h * SSD_HEAD_DIM:(h + 1) * SSD_HEAD_DIM] = 1.0
    expand = jnp.asarray(ex, BF16)
    est = 4 * (tt * (SSD_CONV_CH * 4 + SSD_WIDTH * 2 + 128 * 4 + SSD_WIDTH * 2)
               + (tt + hdr) * SSD_CONV_CH * 4 + tt * SSD_CONV_CH * 4 + 16 * cs * cs * 4)
    return pl.pallas_call(
        functools.partial(_ssd_kernel, tt=tt, cs=cs),
        grid=(batch, nt),
        in_specs=[tile(SSD_CONV_CH), tile(SSD_WIDTH), tile(V7X_LANES),
                  pl.BlockSpec((None, hdr, SSD_CONV_CH), lambda b, t: (b, 0, 0)),
                  pl.BlockSpec((None, npair, V7X_LANES, V7X_LANES), lambda b, t: (b, 0, 0, 0)),
                  per_layer((hdr, SSD_CONV_CH)), per_layer((1, SSD_CONV_CH)), per_layer((1, V7X_LANES)),
                  per_layer((1, V7X_LANES)), per_layer((1, SSD_WIDTH)), per_layer((1, SSD_WIDTH)),
                  const(ltri), const(ident), const(expand)],
        out_specs=[tile(SSD_WIDTH),
                   pl.BlockSpec((None, hdr, SSD_CONV_CH), lambda b, t: (b, 0, 0)),
                   pl.BlockSpec((None, npair, V7X_LANES, V7X_LANES), lambda b, t: (b, 0, 0, 0))],
        out_shape=[jax.ShapeDtypeStruct((batch * seq, SSD_WIDTH), BF16),
                   jax.ShapeDtypeStruct((batch, hdr, SSD_CONV_CH), F32),
                   jax.ShapeDtypeStruct((batch, npair, V7X_LANES, V7X_LANES), F32)],
        scratch_shapes=[pltpu.VMEM((tt + hdr, SSD_CONV_CH), F32), pltpu.VMEM((tt, SSD_CONV_CH), F32),
                        pltpu.VMEM((tt, V7X_LANES), F32), pltpu.VMEM((tt, V7X_LANES), F32),
                        pltpu.VMEM((npair, V7X_LANES, V7X_LANES), F32)],
        compiler_params=pltpu.CompilerParams(
            dimension_semantics=("arbitrary", "arbitrary"), vmem_limit_bytes=_vmem_limit(est)),
        name="ssd",
    )(xbc, z, dtga, conv_prev, s0, consts["conv_w"], consts["conv_b"], consts["dt_bias"], consts["a_log"],
      consts["d_exp"], consts["ssd_norm"], ltri, ident, expand)


def _gla_level_constants(cs):
    nl = int(np.log2(cs))
    idx = np.arange(cs)
    sums = np.zeros(((nl + 2) * cs, cs), np.float32)
    masks = np.zeros((nl + 1, cs, cs), np.float32)
    for lv in range(nl):
        s = 1 << lv
        blk = idx // (2 * s)
        right = (idx // s) % 2 == 1
        m = blk * 2 * s + s - 1
        for i in range(cs):
            if right[i]:
                sums[lv * cs + i, m[i] + 1:i + 1] = 1.0
            else:
                sums[lv * cs + i, i + 1:m[i] + 1] = 1.0
        masks[lv] = (blk[:, None] == blk[None, :]) & right[:, None] & ~right[None, :]
    masks[nl] = np.eye(cs)
    sums[nl * cs:(nl + 1) * cs] = np.tril(np.ones((cs, cs)))
    sums[(nl + 1) * cs:] = np.triu(np.ones((cs, cs)), 1)
    return nl, sums, masks


def _gla_kernel(gqk_ref, gv_ref, gr_ref, dtga_ref, s0_ref, wgate_ref, bgate_ref, norm_ref, sums_ref, masks_ref,
                ones_ref, y_ref, sout_ref, g_scr, st_scr, *, tt, cs, nl):
    t = pl.program_id(1)

    @pl.when(t == 0)
    def _():
        st_scr[...] = s0_ref[...]

    xg = jnp.dot(dtga_ref[...].astype(BF16), wgate_ref[...], preferred_element_type=F32) + bgate_ref[...]
    g_scr[...] = (jnp.minimum(xg, 0.0) - jnp.log1p(jnp.exp(-jnp.abs(xg)))) * (1.0 / GLA_TAU)

    qhead = lax.broadcasted_iota(jnp.int32, (cs, V7X_LANES), 1) // GLA_DK
    vhead = lax.broadcasted_iota(jnp.int32, (cs, GLA_WIDTH), 1) // GLA_DV
    bi = lax.broadcasted_iota(jnp.int32, (GLA_WIDTH, V7X_LANES), 0) // GLA_DV
    bj = lax.broadcasted_iota(jnp.int32, (GLA_WIDTH, V7X_LANES), 1) // GLA_DK
    blockdiag = bi == bj

    def chunk(c, carry):
        r0 = pl.multiple_of(c * cs, cs)
        g = g_scr[pl.ds(r0, cs), :]
        qk = gqk_ref[pl.ds(r0, cs), :]
        q = qk[:, :V7X_LANES] * (GLA_DK ** -0.5)
        k = qk[:, V7X_LANES:]
        v = gv_ref[pl.ds(r0, cs), :]
        e = jnp.exp(_dot_exact_l(sums_ref[...], g))
        att = [None] * GLA_HEADS
        for lv in range(nl + 1):
            if lv < nl:
                e_lv = e[lv * cs:(lv + 1) * cs]
                qh, kh = q * e_lv, (k * e_lv).astype(BF16)
            else:
                qh, kh = q, k.astype(BF16)
            qstack = jnp.concatenate([jnp.where(qhead == h, qh, 0.0) for h in range(GLA_HEADS)], axis=0).astype(BF16)
            out = lax.dot_general(qstack, kh, _NT, preferred_element_type=F32)
            m = masks_ref[lv]
            for h in range(GLA_HEADS):
                part = m * out[h * cs:(h + 1) * cs]
                att[h] = part if att[h] is None else att[h] + part
        e_c = e[nl * cs:(nl + 1) * cs]
        e_rev = e[(nl + 1) * cs:(nl + 2) * cs]
        qt = (q * e_c).astype(BF16)
        kt = (k * e_rev).astype(BF16)
        a_cat = jnp.concatenate([a.astype(BF16) for a in att], axis=1)
        v_bd = jnp.concatenate([jnp.where(vhead == h, v, jnp.zeros_like(v)) for h in range(GLA_HEADS)], axis=0)
        st = st_scr[...]
        o = (jnp.dot(a_cat, v_bd, preferred_element_type=F32)
             + lax.dot_general(qt, st.astype(BF16), _NT, preferred_element_type=F32))
        upd = lax.dot_general(v, kt, _TN, preferred_element_type=F32)
        st_scr[...] = st * e_c[cs - 1:cs, :] + jnp.where(blockdiag, upd, 0.0)
        ms = _dot_exact_r(o * o, ones_ref[...]) * (1.0 / GLA_DV)
        r = gr_ref[pl.ds(r0, cs), :].astype(F32)
        y_ref[pl.ds(r0, cs), :] = (o * lax.rsqrt(ms + EPS) * norm_ref[...] * _silu(r)).astype(BF16)
        return carry

    lax.fori_loop(0, tt // cs, chunk, 0)

    @pl.when(t == pl.num_programs(1) - 1)
    def _():
        sout_ref[...] = st_scr[...]


def _gla_call(gqk, gv, gr, dtga, s0, consts, layer, batch, seq):
    tt = min(SEQ_TILE, seq)
    cs = min(SCAN_CHUNK, seq)
    nt = seq // tt
    nl, sums_np, masks_np = _gla_level_constants(cs)
    sums = jnp.asarray(sums_np, BF16)
    masks = jnp.asarray(masks_np, F32)
    ones = jnp.asarray(np.kron(np.eye(GLA_HEADS), np.ones((GLA_DV, GLA_DV))).astype(np.float32), BF16)
    tile = lambda w: pl.BlockSpec((tt, w), lambda b, t: (b * nt + t, 0))
    per_layer = lambda shape: _resident((None,) + shape, lambda b, t: (layer, 0, 0))
    const = lambda a: _resident(a.shape, lambda b, t: (0,) * a.ndim)
    est = 4 * (tt * (256 * 4 + 256 * 2 + 256 * 2 + 128 * 4 + 256 * 2)
               + 3 * (nl + 2) * cs * 128 * 4 + 12 * cs * cs * 4) + sums.size * 2 + masks.size * 4
    return pl.pallas_call(
        functools.partial(_gla_kernel, tt=tt, cs=cs, nl=nl),
        grid=(batch, nt),
        in_specs=[tile(2 * V7X_LANES), tile(GLA_WIDTH), tile(GLA_WIDTH), tile(V7X_LANES),
                  pl.BlockSpec((None, GLA_WIDTH, V7X_LANES), lambda b, t: (b, 0, 0)),
                  per_layer((V7X_LANES, V7X_LANES)), per_layer((1, V7X_LANES)), per_layer((1, GLA_WIDTH)),
                  const(sums), const(masks), const(ones)],
        out_specs=[tile(GLA_WIDTH), pl.BlockSpec((None, GLA_WIDTH, V7X_LANES), lambda b, t: (b, 0, 0))],
        out_shape=[jax.ShapeDtypeStruct((batch * seq, GLA_WIDTH), BF16),
                   jax.ShapeDtypeStruct((batch, GLA_WIDTH, V7X_LANES), F32)],
        scratch_shapes=[pltpu.VMEM((tt, V7X_LANES), F32), pltpu.VMEM((GLA_WIDTH, V7X_LANES), F32)],
        compiler_params=pltpu.CompilerParams(
            dimension_semantics=("arbitrary", "arbitrary"), vmem_limit_bytes=_vmem_limit(est)),
        name="gla",
    )(gqk, gv, gr, dtga, s0, consts["gla_w_gate"], consts["gla_b_gate"], consts["gla_norm"], sums, masks, ones)


N_MAPS = 2 * DIFF_HEADS


def _masked_queries(q):
    group = lax.broadcasted_iota(jnp.int32, q.shape, 1) // DIFF_DK
    return [jnp.where(group == hm, q, jnp.zeros_like(q)) for hm in range(N_MAPS)]


def _values_with_ones(v):
    left = lax.broadcasted_iota(jnp.int32, (v.shape[0], V7X_LANES), 1) < DIFF_DV
    out = []
    for pr in range(DIFF_HEADS // 2):
        vp = v[:, V7X_LANES * pr:V7X_LANES * (pr + 1)]
        out.append(jnp.where(left, vp, 1.0).astype(BF16))
        out.append(jnp.where(left, pltpu.roll(vp, DIFF_DV, axis=1), 1.0).astype(BF16))
    return out


def _softmax_step(qm, k, vh, m_prev, acc_prev, mask):
    s = lax.dot_general(qm, k, _NT, preferred_element_type=F32)
    if mask is not None:
        s = jnp.where(mask, s, -jnp.inf)
    m_new = jnp.maximum(m_prev, jnp.max(s, axis=-1, keepdims=True))
    alpha = jnp.exp2(m_prev - m_new)
    reps = s.shape[1] // V7X_LANES
    m_wide = jnp.concatenate([m_new] * reps, axis=1) if reps > 1 else m_new[:, :s.shape[1]]
    p = jnp.exp2(s - m_wide).astype(BF16)
    acc = alpha * acc_prev + jnp.dot(p, vh, preferred_element_type=F32)
    return m_new, acc


def _diff_finalize(accs, lam_ref, norm_ref, lam_init):
    lq = lam_ref[...]
    s1 = jnp.sum(lq[0:1, :] * lq[1:2, :], axis=-1, keepdims=True)
    s2 = jnp.sum(lq[2:3, :] * lq[3:4, :], axis=-1, keepdims=True)
    lam = jnp.exp(s1) - jnp.exp(s2) + lam_init
    tq = accs[0].shape[0]
    left = lax.broadcasted_iota(jnp.int32, (tq, V7X_LANES), 1) < DIFF_DV
    ys = []
    for h in range(DIFF_HEADS):
        a1, a2 = accs[2 * h], accs[2 * h + 1]
        o = a1 / pltpu.roll(a1, DIFF_DV, axis=1) - lam * (a2 / pltpu.roll(a2, DIFF_DV, axis=1))
        o = jnp.where(left, o, 0.0)
        ms = jnp.sum(o * o, axis=-1, keepdims=True) * (1.0 / DIFF_DV)
        ys.append(o * lax.rsqrt(ms + EPS) * norm_ref[...] * (1.0 - lam_init))
    pairs = [jnp.where(left, ys[2 * p], pltpu.roll(ys[2 * p + 1], DIFF_DV, axis=1)) for p in range(DIFF_HEADS // 2)]
    return jnp.concatenate(pairs, axis=1)


def _attn_prompt_kernel(qi_ref, ki_ref, q_ref, k_ref, v_ref, lam_ref, norm_ref, o_ref, qm_scr, m_scr, acc_scr,
                        *, tile, lam_init):
    p = pl.program_id(1)
    qi, ki = qi_ref[p], ki_ref[p]

    @pl.when(ki == 0)
    def _():
        for hm, qm in enumerate(_masked_queries(q_ref[...])):
            qm_scr[hm] = qm
        m_scr[...] = jnp.full(m_scr.shape, -jnp.inf, F32)
        acc_scr[...] = jnp.zeros(acc_scr.shape, F32)

    def step(mask):
        k = k_ref[...].astype(BF16)
        vhs = _values_with_ones(v_ref[...])
        for hm in range(N_MAPS):
            m_new, acc = _softmax_step(qm_scr[hm], k, vhs[hm // 2], m_scr[hm], acc_scr[hm], mask)
            m_scr[hm] = m_new
            acc_scr[hm] = acc

    @pl.when(ki < qi)
    def _():
        step(None)

    @pl.when(ki == qi)
    def _():
        ri = lax.broadcasted_iota(jnp.int32, (tile, tile), 0) // CHUNK
        ci = lax.broadcasted_iota(jnp.int32, (tile, tile), 1) // CHUNK
        step(ci <= ri)
        accs = [acc_scr[hm] for hm in range(N_MAPS)]
        o_ref[...] = _diff_finalize(accs, lam_ref, norm_ref, lam_init).astype(BF16)


def _attn_prompt_call(q, k, v, lam_rows, norm, layer, batch, seq, lam_init):
    tile = min(ATTN_TILE, seq)
    nq = seq // tile
    pairs = [(a, b) for a in range(nq) for b in range(a + 1)]
    qi = jnp.asarray([a for a, _ in pairs], jnp.int32)
    ki = jnp.asarray([b for _, b in pairs], jnp.int32)
    qspec = pl.BlockSpec((tile, DIFF_WIDTH), lambda b, p, qi, ki: (b * nq + qi[p], 0))
    kspec = pl.BlockSpec((tile, DIFF_WIDTH), lambda b, p, qi, ki: (b * nq + ki[p], 0))
    per_layer = lambda shape: pl.BlockSpec((None,) + shape, lambda b, p, qi, ki: (layer, 0, 0))
    est = (2 * tile * DIFF_WIDTH * (2 + 4 + 4 + 2) + N_MAPS * tile * (DIFF_WIDTH * 2 + 2 * V7X_LANES * 4)
           + 6 * tile * tile * 4)
    return pl.pallas_call(
        functools.partial(_attn_prompt_kernel, tile=tile, lam_init=lam_init),
        grid_spec=pltpu.PrefetchScalarGridSpec(
            num_scalar_prefetch=2,
            grid=(batch, len(pairs)),
            in_specs=[qspec, kspec, kspec, per_layer((V7X_SUBLANES, V7X_LANES)), per_layer((1, V7X_LANES))],
            out_specs=qspec,
            scratch_shapes=[pltpu.VMEM((N_MAPS, tile, DIFF_WIDTH), BF16),
                            pltpu.VMEM((N_MAPS, tile, V7X_LANES), F32),
                            pltpu.VMEM((N_MAPS, tile, V7X_LANES), F32)]),
        out_shape=jax.ShapeDtypeStruct((batch * seq, DIFF_WIDTH), BF16),
        compiler_params=pltpu.CompilerParams(
            dimension_semantics=("arbitrary", "arbitrary"), vmem_limit_bytes=_vmem_limit(est)),
        name="attn_prompt",
    )(qi, ki, q, k, v, lam_rows, norm)


def _attn_sample_kernel(q_ref, k_ref, v_ref, kp_ref, vp_ref, lam_ref, norm_ref, o_ref, *, lam_init):
    qms = _masked_queries(q_ref[...])
    k_past, k_new = kp_ref[...].astype(BF16), k_ref[...].astype(BF16)
    vh_past, vh_new = _values_with_ones(vp_ref[...]), _values_with_ones(v_ref[...])
    tq = q_ref.shape[0]
    accs = []
    for hm in range(N_MAPS):
        m0 = jnp.full((tq, V7X_LANES), -jnp.inf, F32)
        a0 = jnp.zeros((tq, V7X_LANES), F32)
        m1, a1 = _softmax_step(qms[hm], k_past, vh_past[hm // 2], m0, a0, None)
        _, a2 = _softmax_step(qms[hm], k_new, vh_new[hm // 2], m1, a1, None)
        accs.append(a2)
    o_ref[...] = _diff_finalize(accs, lam_ref, norm_ref, lam_init).astype(BF16)


def _attn_sample_call(q, k, v, k_past, v_past, lam_rows, norm, layer, batch, seq, lam_init):
    past = k_past.shape[2]
    new = pl.BlockSpec((seq, DIFF_WIDTH), lambda b: (b, 0))
    old = pl.BlockSpec((None, None, past, DIFF_WIDTH), lambda b: (layer, b, 0, 0))
    per_layer = lambda shape: pl.BlockSpec((None,) + shape, lambda b: (layer, 0, 0))
    est = 4 * past * DIFF_WIDTH * 4 + 3 * past * DIFF_WIDTH * 2 + 8 * seq * past * 4
    return pl.pallas_call(
        functools.partial(_attn_sample_kernel, lam_init=lam_init),
        grid=(batch,),
        in_specs=[new, new, new, old, old, per_layer((V7X_SUBLANES, V7X_LANES)), per_layer((1, V7X_LANES))],
        out_specs=new,
        out_shape=jax.ShapeDtypeStruct((batch * seq, DIFF_WIDTH), BF16),
        compiler_params=pltpu.CompilerParams(
            dimension_semantics=("arbitrary",), vmem_limit_bytes=_vmem_limit(est)),
        name="attn_sample",
    )(q, k, v, k_past, v_past, lam_rows, norm)


def _pad_lanes(a, width):
    return jnp.pad(a, [(0, 0)] * (a.ndim - 1) + [(0, width - a.shape[-1])])


def _prepare_params(p):
    cols, start = [], 0
    for s in IN_SPLITS:
        cols.append(p["w_in"][..., start:start + s])
        start += s
    z, xbc, dt, gq, gk, gv, gr, ga, aq, ak, av = cols
    zeros = jnp.zeros(p["w_in"].shape[:2] + (_PAD_SMALL,), p["w_in"].dtype)
    w_in = jnp.concatenate([z, xbc, gq, gk, gv, gr, aq, ak, av, dt, ga, zeros], axis=-1).astype(BF16)
    row = lambda a: a[:, None, :]
    w_gate = jnp.zeros((DEPTH, V7X_LANES, V7X_LANES), F32)
    w_gate = w_gate.at[:, SSD_HEADS:SSD_HEADS + GLA_LOWRANK, :].set(p["gla_w_gate"]).astype(BF16)
    lam_rows = jnp.stack([p["diff_lambda_q1"], p["diff_lambda_k1"], p["diff_lambda_q2"], p["diff_lambda_k2"]], axis=1)
    lam_rows = jnp.pad(lam_rows, ((0, 0), (0, V7X_SUBLANES - 4), (0, V7X_LANES - DIFF_DK)))
    return dict(
        ffn1_norm=row(p["ffn1_norm"]), ffn2_norm=row(p["ffn2_norm"]), mix_norm=row(p["mix_norm"]),
        final_norm=p["final_norm"][None, :],
        ffn1=(p["ffn1_w_gate"].astype(BF16), p["ffn1_w_up"].astype(BF16), p["ffn1_w_down"].astype(BF16)),
        ffn2=(p["ffn2_w_gate"].astype(BF16), p["ffn2_w_up"].astype(BF16), p["ffn2_w_down"].astype(BF16)),
        w_in=w_in, w_out=p["w_out"].astype(BF16),
        conv_w=jnp.pad(p["ssd_conv_w"], ((0, 0), (0, V7X_SUBLANES - SSD_CONV), (0, 0))),
        conv_b=row(p["ssd_conv_b"]),
        dt_bias=row(_pad_lanes(p["ssd_dt_bias"], V7X_LANES)), a_log=row(_pad_lanes(p["ssd_a_log"], V7X_LANES)),
        d_exp=row(jnp.repeat(p["ssd_d"], SSD_HEAD_DIM, axis=-1)), ssd_norm=row(p["ssd_norm"]),
        gla_w_gate=w_gate, gla_b_gate=row(p["gla_b_gate"]), gla_norm=row(jnp.tile(p["gla_norm"], (1, GLA_HEADS))),
        lam_rows=lam_rows, diff_norm=row(jnp.tile(p["diff_norm"], (1, V7X_LANES // DIFF_DV))),
    )


def _ssd_state_to_pairs(s):
    b = s.shape[0]
    st = jnp.swapaxes(s, -1, -2).reshape(b, SSD_HEADS // 2, 2, SSD_STATE, SSD_HEAD_DIM)
    eye = jnp.eye(2, dtype=s.dtype)
    bd = st[:, :, :, :, None, :] * eye[None, None, :, None, :, None]
    return bd.reshape(b, SSD_HEADS // 2, 2 * SSD_STATE, 2 * SSD_HEAD_DIM)


def _ssd_state_from_pairs(bd):
    b = bd.shape[0]
    x = bd.reshape(b, SSD_HEADS // 2, 2, SSD_STATE, 2, SSD_HEAD_DIM)
    st = jnp.stack([x[:, :, 0, :, 0, :], x[:, :, 1, :, 1, :]], axis=2)
    return jnp.swapaxes(st.reshape(b, SSD_HEADS, SSD_STATE, SSD_HEAD_DIM), -1, -2)


def _gla_state_to_blocks(s):
    b = s.shape[0]
    st = jnp.swapaxes(s, -1, -2)
    eye = jnp.eye(GLA_HEADS, dtype=s.dtype)
    bd = st[:, :, :, None, :] * eye[None, :, None, :, None]
    return bd.reshape(b, GLA_HEADS * GLA_DV, GLA_HEADS * GLA_DK)


def _gla_state_from_blocks(bd):
    b = bd.shape[0]
    x = bd.reshape(b, GLA_HEADS, GLA_DV, GLA_HEADS, GLA_DK)
    st = jnp.stack([x[:, h, :, h, :] for h in range(GLA_HEADS)], axis=1)
    return jnp.swapaxes(st, -1, -2)


def _trunk(x, conv_st, ssd_st, gla_st, k_cache, v_cache, pp):
    batch, seq, d = x.shape
    prompt = k_cache is None
    x = x.reshape(batch * seq, d)
    convs, ssds, glas, ks, vs = [], [], [], [], []
    hdr = V7X_SUBLANES
    for l in range(DEPTH):
        lam_init = 0.8 - 0.6 * float(np.exp(-0.3 * l))
        x = _ffn_call(x, l, pp["ffn1_norm"], *pp["ffn1"])
        z, xbc, gqk, gv, gr, aq, ak, av, dtga = _inproj_call(x, l, pp["mix_norm"], pp["w_in"])
        if prompt:
            conv_prev = jnp.zeros((batch, hdr, SSD_CONV_CH), F32)
            ssd0 = jnp.zeros((batch, SSD_HEADS // 2, V7X_LANES, V7X_LANES), F32)
            gla0 = jnp.zeros((batch, GLA_WIDTH, V7X_LANES), F32)
        else:
            conv_prev = jnp.pad(conv_st[l], ((0, 0), (hdr - (SSD_CONV - 1), 0), (0, 0)))
            ssd0 = _ssd_state_to_pairs(ssd_st[l].astype(F32))
            gla0 = _gla_state_to_blocks(gla_st[l].astype(F32))
        y_ssd, conv_out, ssd_out = _ssd_call(xbc, z, dtga, conv_prev, ssd0, pp, l, batch, seq)
        y_gla, gla_out = _gla_call(gqk, gv, gr, dtga, gla0, pp, l, batch, seq)
        if prompt:
            y_diff = _attn_prompt_call(aq, ak, av, pp["lam_rows"], pp["diff_norm"], l, batch, seq, lam_init)
        else:
            past = k_cache.shape[2]
            y_diff = _attn_sample_call(aq, ak, av, k_cache.reshape(DEPTH, batch, past, DIFF_WIDTH),
                                       v_cache.reshape(DEPTH, batch, past, DIFF_WIDTH),
                                       pp["lam_rows"], pp["diff_norm"], l, batch, seq, lam_init)
        last = l == DEPTH - 1
        x = _ffn_call(x, l, pp["ffn2_norm"], *pp["ffn2"], mix=(y_ssd, y_gla, y_diff, pp["w_out"]),
                      final_norm=pp["final_norm"] if last else None)
        convs.append(conv_out[:, hdr - (SSD_CONV - 1):, :])
        ssds.append(_ssd_state_from_pairs(ssd_out))
        glas.append(_gla_state_from_blocks(gla_out))
        ks.append(ak.reshape(batch, seq, DIFF_HEADS, 2 * DIFF_DK))
        vs.append(av.reshape(batch, seq, DIFF_HEADS, DIFF_DV))
    return (x.reshape(batch, seq, d), jnp.stack(convs), jnp.stack(ssds), jnp.stack(glas),
            jnp.stack(ks), jnp.stack(vs))


def kernel(x_prompt, x_sample, state_ssd_conv, state_ssd, state_gla, cache_diff_k, cache_diff_v, ffn1_norm, ffn1_w_gate, ffn1_w_up, ffn1_w_down, mix_norm, w_in, w_out, ssd_conv_w, ssd_conv_b, ssd_dt_bias, ssd_a_log, ssd_d, ssd_norm, gla_w_gate, gla_b_gate, gla_norm, diff_lambda_q1, diff_lambda_k1, diff_lambda_q2, diff_lambda_k2, diff_norm, ffn2_norm, ffn2_w_gate, ffn2_w_up, ffn2_w_down, final_norm):
    pp = _prepare_params(dict(
        ffn1_norm=ffn1_norm, ffn1_w_gate=ffn1_w_gate, ffn1_w_up=ffn1_w_up, ffn1_w_down=ffn1_w_down,
        mix_norm=mix_norm, w_in=w_in, w_out=w_out, ssd_conv_w=ssd_conv_w, ssd_conv_b=ssd_conv_b,
        ssd_dt_bias=ssd_dt_bias, ssd_a_log=ssd_a_log, ssd_d=ssd_d, ssd_norm=ssd_norm,
        gla_w_gate=gla_w_gate, gla_b_gate=gla_b_gate, gla_norm=gla_norm,
        diff_lambda_q1=diff_lambda_q1, diff_lambda_k1=diff_lambda_k1, diff_lambda_q2=diff_lambda_q2,
        diff_lambda_k2=diff_lambda_k2, diff_norm=diff_norm,
        ffn2_norm=ffn2_norm, ffn2_w_gate=ffn2_w_gate, ffn2_w_up=ffn2_w_up, ffn2_w_down=ffn2_w_down,
        final_norm=final_norm))
    y_p, conv_p, ssd_p, gla_p, k_p, v_p = _trunk(x_prompt, None, None, None, None, None, pp)
    y_s, conv_s, ssd_s, gla_s, k_s, v_s = _trunk(
        x_sample, state_ssd_conv, state_ssd, state_gla, cache_diff_k, cache_diff_v, pp)
    return (y_p, y_s, conv_p, conv_s, ssd_p, ssd_s, gla_p, gla_s, k_p, k_s, v_p, v_s)
```

```python
import functools

import jax
import jax.numpy as jnp
import numpy as np
from jax import lax
from jax.experimental import pallas as pl
from jax.experimental.pallas import tpu as pltpu

F32 = jnp.float32
BF16 = jnp.bfloat16

EPS = 1e-5
DEPTH = 4
D_MODEL = 1024
D_FF = 2816
CHUNK = 64

SSD_WIDTH = 512
SSD_HEADS = 8
SSD_HEAD_DIM = 64
SSD_GROUPS = 2
SSD_STATE = 64
SSD_CONV = 4
SSD_CONV_CH = SSD_WIDTH + 2 * SSD_GROUPS * SSD_STATE

GLA_WIDTH = 256
GLA_HEADS = 4
GLA_DK = 32
GLA_DV = 64
GLA_LOWRANK = 16
GLA_TAU = 16.0

DIFF_WIDTH = 256
DIFF_HEADS = 4
DIFF_DK = 32
DIFF_DV = 64

IN_SPLITS = (SSD_WIDTH, SSD_CONV_CH, SSD_HEADS,
             GLA_HEADS * GLA_DK, GLA_HEADS * GLA_DK, GLA_WIDTH, GLA_WIDTH, GLA_LOWRANK,
             DIFF_HEADS * 2 * DIFF_DK, DIFF_HEADS * 2 * DIFF_DK, DIFF_WIDTH)

V7X_LANES = 128
V7X_SUBLANES = 8
V7X_VMEM_BYTES = 64 * 1024 * 1024

_PAD_SMALL = V7X_LANES - SSD_HEADS - GLA_LOWRANK
IN_PADDED = sum(IN_SPLITS) + _PAD_SMALL
O_Z, O_XBC, O_GQ, O_GV, O_GR, O_AQ, O_AK, O_AV, O_DTGA = 0, 512, 1280, 1536, 1792, 2048, 2304, 2560, 2816

TM = 512
SEQ_TILE = 512
SCAN_CHUNK = 128
ATTN_TILE = 512

_NT = (((1,), (1,)), ((), ()))
_TN = (((0,), (0,)), ((), ()))


def _vmem_limit(nbytes):
    return int(min(max(nbytes, 16 * 1024 * 1024), V7X_VMEM_BYTES - 6 * 1024 * 1024))


def _rmsnorm(x, g):
    ms = jnp.mean(x * x, axis=-1, keepdims=True)
    return x * lax.rsqrt(ms + EPS) * g


def _silu(x):
    return x * jax.nn.sigmoid(x)


def _split3(x):
    hi = x.astype(BF16)
    r = x - hi.astype(F32)
    mid = r.astype(BF16)
    lo = (r - mid.astype(F32)).astype(BF16)
    return hi, mid, lo


def _dot_exact_l(a, x):
    return sum(jnp.dot(a, p, preferred_element_type=F32) for p in _split3(x))


def _dot_exact_r(x, a):
    return sum(jnp.dot(p, a, preferred_element_type=F32) for p in _split3(x))


def _dot_exact_nt(a, x):
    return sum(lax.dot_general(a, p, _NT, preferred_element_type=F32) for p in _split3(x))


def _resident(block_shape, index_map):
    return pl.BlockSpec(block_shape, index_map, pipeline_mode=pl.Buffered(1))


V7X_MXU_DIM = 256
_FF_CUT = (D_FF // V7X_MXU_DIM + 1) // 2 * V7X_MXU_DIM
FF_BLOCKS = ((0, _FF_CUT), (_FF_CUT, D_FF))


def _ffn_kernel(*refs, with_mix, with_final):
    it = iter(refs)
    x_ref = next(it)
    if with_mix:
        y1_ref, y2_ref, y3_ref, wo_ref = next(it), next(it), next(it), next(it)
    g_ref, wg_ref, wu_ref, wd_ref = next(it), next(it), next(it), next(it)
    if with_final:
        fn_ref = next(it)
    o_ref = next(it)

    x = x_ref[...]
    if with_mix:
        y = jnp.concatenate([y1_ref[...], y2_ref[...], y3_ref[...]], axis=-1)
        x = x + jnp.dot(y, wo_ref[...], preferred_element_type=F32)
    h = _rmsnorm(x, g_ref[...]).astype(BF16)
    acc = None
    for lo, hi in FF_BLOCKS:
        gate = jnp.dot(h, wg_ref[:, lo:hi], preferred_element_type=F32)
        up = jnp.dot(h, wu_ref[:, lo:hi], preferred_element_type=F32)
        act = (_silu(gate) * up).astype(BF16)
        part = jnp.dot(act, wd_ref[lo:hi, :], preferred_element_type=F32)
        acc = part if acc is None else acc + part
    x = x + 0.5 * acc
    if with_final:
        x = _rmsnorm(x, fn_ref[...])
    o_ref[...] = x


def _ffn_call(x, layer, norm, wg, wu, wd, mix=None, final_norm=None):
    t, d = x.shape
    tm = min(TM, t)
    tile = lambda w: pl.BlockSpec((tm, w), lambda i: (i, 0))
    per_layer = lambda shape: _resident((None,) + shape, lambda i: (layer, 0, 0))
    args, specs = [x], [tile(d)]
    if mix is not None:
        y1, y2, y3, wo = mix
        args += [y1, y2, y3, wo]
        specs += [tile(y1.shape[1]), tile(y2.shape[1]), tile(y3.shape[1]), per_layer((d, d))]
    args += [norm, wg, wu, wd]
    specs += [per_layer((1, d)), per_layer((d, D_FF)), per_layer((d, D_FF)), per_layer((D_FF, d))]
    if final_norm is not None:
        args.append(final_norm)
        specs.append(_resident((1, d), lambda i: (0, 0)))
    weights = 2 * (3 * d * D_FF + d * d)
    tiles = 2 * (2 * tm * d * 4 + tm * d * 2)
    temps = tm * (3 * d * 4 + _FF_CUT * (4 + 4 + 2))
    return pl.pallas_call(
        functools.partial(_ffn_kernel, with_mix=mix is not None, with_final=final_norm is not None),
        grid=(t // tm,),
        in_specs=specs,
        out_specs=tile(d),
        out_shape=jax.ShapeDtypeStruct((t, d), F32),
        compiler_params=pltpu.CompilerParams(
            dimension_semantics=("arbitrary",), vmem_limit_bytes=_vmem_limit(weights + tiles + 2 * temps)),
        name="ffn",
    )(*args)


def _inproj_kernel(x_ref, g_ref, w_ref, *refs):
    z_ref, xbc_ref, gqk_ref, gv_ref, gr_ref, aq_ref, ak_ref, av_ref, dtga_ref = refs[-9:]
    h = _rmsnorm(x_ref[...], g_ref[...]).astype(BF16)
    u = jnp.dot(h, w_ref[...], preferred_element_type=F32)
    z_ref[...] = u[:, O_Z:O_XBC].astype(BF16)
    xbc_ref[...] = u[:, O_XBC:O_GQ]
    gqk_ref[...] = u[:, O_GQ:O_GV]
    gv_ref[...] = u[:, O_GV:O_GR].astype(BF16)
    gr_ref[...] = u[:, O_GR:O_AQ].astype(BF16)
    aq_ref[...] = (u[:, O_AQ:O_AK] * (DIFF_DK ** -0.5 * np.log2(np.e))).astype(BF16)
    ak_ref[...] = u[:, O_AK:O_AV]
    av_ref[...] = u[:, O_AV:O_DTGA]
    dtga_ref[...] = u[:, O_DTGA:IN_PADDED]


def _inproj_call(x, layer, norm, w, kv_bufs=None):
    t, d = x.shape
    widths = [(O_XBC - O_Z, BF16), (O_GQ - O_XBC, F32), (O_GV - O_GQ, F32), (O_GR - O_GV, BF16),
              (O_AQ - O_GR, BF16), (O_AK - O_AQ, BF16), (O_AV - O_AK, F32), (O_DTGA - O_AV, F32),
              (IN_PADDED - O_DTGA, F32)]
    kv_out = (6, 7)
    tm = min(TM, t)
    tile = lambda w_: pl.BlockSpec((tm, w_), lambda i: (i, 0))
    per_layer = lambda shape: _resident((None,) + shape, lambda i: (layer, 0, 0))
    out_specs = [tile(w_) for w_, _ in widths]
    out_shape = [jax.ShapeDtypeStruct((t, w_), dt) for w_, dt in widths]
    for o in kv_out:
        out_specs[o] = pl.BlockSpec((None, tm, widths[o][0]), lambda i: (layer, i, 0))
        out_shape[o] = jax.ShapeDtypeStruct((DEPTH, t, widths[o][0]), F32)
    args = [x, norm, w]
    in_specs = [tile(d), per_layer((1, d)), per_layer((d, IN_PADDED))]
    aliases = {}
    if kv_bufs is not None:
        for buf, o in zip(kv_bufs, kv_out):
            aliases[len(args)] = o
            args.append(buf)
            in_specs.append(pl.BlockSpec(memory_space=pl.ANY))
    est = 2 * d * IN_PADDED + 2 * tm * d * 4 + 3 * tm * IN_PADDED * 4
    return pl.pallas_call(
        _inproj_kernel,
        grid=(t // tm,),
        in_specs=in_specs,
        out_specs=out_specs,
        out_shape=out_shape,
        input_output_aliases=aliases,
        compiler_params=pltpu.CompilerParams(
            dimension_semantics=("arbitrary",), vmem_limit_bytes=_vmem_limit(est)),
        name="inproj",
    )(*args)


def _ssd_kernel(xbc_ref, z_ref, dtga_ref, cprev_ref, s0_ref, cw_ref, cb_ref, dtb_ref, alog_ref, dexp_ref,
                norm_ref, ltri_ref, ident_ref, expand_ref,
                y_ref, cout_ref, sout_ref,
                xp_scr, conv_scr, a_scr, dt_scr, s_scr, *, tt, cs):
    t = pl.program_id(1)
    hdr = V7X_SUBLANES

    @pl.when(t == 0)
    def _():
        xp_scr[0:hdr, :] = cprev_ref[...]
        s_scr[...] = s0_ref[...]

    xp_scr[hdr:hdr + tt, :] = xbc_ref[...]
    conv = cb_ref[...]
    for i in range(SSD_CONV):
        off = hdr - (SSD_CONV - 1) + i
        conv = conv + xp_scr[off:off + tt, :] * cw_ref[i:i + 1, :]
    conv_scr[...] = _silu(conv)
    xp_scr[0:hdr, :] = xp_scr[tt:tt + hdr, :]

    lane = lax.broadcasted_iota(jnp.int32, (tt, V7X_LANES), 1)
    x_dt = dtga_ref[...] + dtb_ref[...]
    dt = jnp.maximum(x_dt, 0.0) + jnp.log1p(jnp.exp(-jnp.abs(x_dt)))
    dt = jnp.where(lane < SSD_HEADS, dt, 0.0)
    dt_scr[...] = dt
    a_scr[...] = dt * (-jnp.exp(alog_ref[...]))

    left = lax.broadcasted_iota(jnp.int32, (cs, V7X_LANES), 1) < SSD_STATE
    ri = lax.broadcasted_iota(jnp.int32, (cs, cs), 0)
    ci = lax.broadcasted_iota(jnp.int32, (cs, cs), 1)
    causal = ci <= ri
    bi = lax.broadcasted_iota(jnp.int32, (V7X_LANES, V7X_LANES), 0) // SSD_STATE
    bj = lax.broadcasted_iota(jnp.int32, (V7X_LANES, V7X_LANES), 1) // SSD_HEAD_DIM
    blockdiag = bi == bj

    def chunk(c, carry):
        r0 = pl.multiple_of(c * cs, cs)
        conv_c = conv_scr[pl.ds(r0, cs), :]
        xs = conv_c[:, :SSD_WIDTH]
        bm = conv_c[:, SSD_WIDTH:SSD_WIDTH + V7X_LANES]
        cm = conv_c[:, SSD_WIDTH + V7X_LANES:]
        a_c = a_scr[pl.ds(r0, cs), :]
        dt_c = dt_scr[pl.ds(r0, cs), :]
        cum = _dot_exact_l(ltri_ref[...], a_c)
        cum_t = _dot_exact_nt(ident_ref[...], cum)
        last = cum[cs - 1:cs, :]
        e_cum = jnp.exp(cum)
        e_rev = jnp.exp(last - cum)
        small = jnp.concatenate([dt_c, e_cum, e_rev], axis=0).astype(BF16)
        wide = jnp.dot(small, expand_ref[...], preferred_element_type=F32)
        dt_x, ecum_x, erev_x = wide[0:cs], wide[cs:2 * cs], wide[2 * cs:3 * cs]
        elast_x = _dot_exact_r(jnp.broadcast_to(e_cum[cs - 1:cs, :], (V7X_SUBLANES, V7X_LANES)),
                               expand_ref[...])[0:1, :]
        xd = xs * dt_x
        bm_sw = pltpu.roll(bm, SSD_STATE, axis=1)
        cm_sw = pltpu.roll(cm, SSD_STATE, axis=1)
        ys = []
        for g in range(SSD_GROUPS):
            if g == 0:
                bdup, cdup, cmask = jnp.where(left, bm, bm_sw), jnp.where(left, cm, cm_sw), jnp.where(left, cm, 0.0)
            else:
                bdup, cdup, cmask = jnp.where(left, bm_sw, bm), jnp.where(left, cm_sw, cm), jnp.where(left, 0.0, cm)
            gmat = lax.dot_general(cmask.astype(BF16), bm.astype(BF16), _NT, preferred_element_type=F32)
            for pr in (2 * g, 2 * g + 1):
                sl = slice(V7X_LANES * pr, V7X_LANES * (pr + 1))
                xd_p = xd[:, sl]
                top = jnp.where(left, xd_p, 0.0).astype(BF16)
                bot = jnp.where(left, 0.0, xd_p).astype(BF16)
                scores = []
                for hd in range(2):
                    h = 2 * pr + hd
                    arg = cum[:, h:h + 1] - cum_t[h:h + 1, :]
                    decay = jnp.where(causal, jnp.exp(jnp.minimum(arg, 0.0)), 0.0)
                    scores.append((gmat * decay).astype(BF16))
                ce = (cdup * ecum_x[:, sl]).astype(BF16)
                s_pair = s_scr[pr]
                lhs = jnp.concatenate([scores[0], scores[1], ce], axis=1)
                rhs = jnp.concatenate([top, bot, s_pair.astype(BF16)], axis=0)
                ys.append(jnp.dot(lhs, rhs, preferred_element_type=F32))
                bt = (bdup * erev_x[:, sl]).astype(BF16)
                upd = lax.dot_general(bt, xd_p.astype(BF16), _TN, preferred_element_type=F32)
                s_scr[pr] = s_pair * elast_x[:, sl] + jnp.where(blockdiag, upd, 0.0)
        y = jnp.concatenate(ys, axis=1) + dexp_ref[...] * xs
        y = y * _silu(z_ref[pl.ds(r0, cs), :].astype(F32))
        y_ref[pl.ds(r0, cs), :] = _rmsnorm(y, norm_ref[...]).astype(BF16)
        return carry

    lax.fori_loop(0, tt // cs, chunk, 0, unroll=True)

    @pl.when(t == pl.num_programs(1) - 1)
    def _():
        cout_ref[...] = xp_scr[0:hdr, :]
        sout_ref[...] = s_scr[...]


def _ssd_call(xbc, z, dtga, conv_prev, s0, consts, layer, batch, seq):
    tt = min(SEQ_TILE, seq)
    cs = min(SCAN_CHUNK, seq)
    nt = seq // tt
    hdr = V7X_SUBLANES
    tile = lambda w: pl.BlockSpec((tt, w), lambda b, t: (b * nt + t, 0))
    per_layer = lambda shape: _resident((None,) + shape, lambda b, t: (layer, 0, 0))
    const = lambda a: _resident(a.shape, lambda b, t: (0,) * a.ndim)
    npair = SSD_HEADS // 2
    ltri = jnp.asarray(np.tril(np.ones((cs, cs), np.float32)), BF16)
    ident = jnp.asarray(np.eye(V7X_LANES, dtype=np.float32), BF16)
    ex = np.zeros((V7X_LANES, SSD_WIDTH), np.float32)
    for h in range(SSD_HEADS):
        ex[h, h * SSD_HEAD_DIM:(h + 1) * SSD_HEAD_DIM] = 1.0
    expand = jnp.asarray(ex, BF16)
    est = 4 * (tt * (SSD_CONV_CH * 4 + SSD_WIDTH * 2 + 128 * 4 + SSD_WIDTH * 2)
               + (tt + hdr) * SSD_CONV_CH * 4 + tt * SSD_CONV_CH * 4 + 16 * cs * cs * 4)
    return pl.pallas_call(
        functools.partial(_ssd_kernel, tt=tt, cs=cs),
        grid=(batch, nt),
        in_specs=[tile(SSD_CONV_CH), tile(SSD_WIDTH), tile(V7X_LANES),
                  pl.BlockSpec((None, hdr, SSD_CONV_CH), lambda b, t: (b, 0, 0)),
                  pl.BlockSpec((None, npair, V7X_LANES, V7X_LANES), lambda b, t: (b, 0, 0, 0)),
                  per_layer((hdr, SSD_CONV_CH)), per_layer((1, SSD_CONV_CH)), per_layer((1, V7X_LANES)),
                  per_layer((1, V7X_LANES)), per_layer((1, SSD_WIDTH)), per_layer((1, SSD_WIDTH)),
                  const(ltri), const(ident), const(expand)],
        out_specs=[tile(SSD_WIDTH),
                   pl.BlockSpec((None, hdr, SSD_CONV_CH), lambda b, t: (b, 0, 0)),
                   pl.BlockSpec((None, npair, V7X_LANES, V7X_LANES), lambda b, t: (b, 0, 0, 0))],
        out_shape=[jax.ShapeDtypeStruct((batch * seq, SSD_WIDTH), BF16),
                   jax.ShapeDtypeStruct((batch, hdr, SSD_CONV_CH), F32),
                   jax.ShapeDtypeStruct((batch, npair, V7X_LANES, V7X_LANES), F32)],
        scratch_shapes=[pltpu.VMEM((tt + hdr, SSD_CONV_CH), F32), pltpu.VMEM((tt, SSD_CONV_CH), F32),
                        pltpu.VMEM((tt, V7X_LANES), F32), pltpu.VMEM((tt, V7X_LANES), F32),
                        pltpu.VMEM((npair, V7X_LANES, V7X_LANES), F32)],
        compiler_params=pltpu.CompilerParams(
            dimension_semantics=("arbitrary", "arbitrary"), vmem_limit_bytes=_vmem_limit(est)),
        name="ssd",
    )(xbc, z, dtga, conv_prev, s0, consts["conv_w"], consts["conv_b"], consts["dt_bias"], consts["a_log"],
      consts["d_exp"], consts["ssd_norm"], ltri, ident, expand)


def _gla_level_constants(cs):
    nl = int(np.log2(cs))
    idx = np.arange(cs)
    sums = np.zeros(((nl + 2) * cs, cs), np.float32)
    masks = np.zeros((nl + 1, cs, cs), np.float32)
    for lv in range(nl):
        s = 1 << lv
        blk = idx // (2 * s)
        right = (idx // s) % 2 == 1
        m = blk * 2 * s + s - 1
        for i in range(cs):
            if right[i]:
                sums[lv * cs + i, m[i] + 1:i + 1] = 1.0
            else:
                sums[lv * cs + i, i + 1:m[i] + 1] = 1.0
        masks[lv] = (blk[:, None] == blk[None, :]) & right[:, None] & ~right[None, :]
    masks[nl] = np.eye(cs)
    sums[nl * cs:(nl + 1) * cs] = np.tril(np.ones((cs, cs)))
    sums[(nl + 1) * cs:] = np.triu(np.ones((cs, cs)), 1)
    return nl, sums, masks


def _gla_kernel(gqk_ref, gv_ref, gr_ref, dtga_ref, s0_ref, wgate_ref, bgate_ref, norm_ref, sums_ref, masks_ref,
                ones_ref, y_ref, sout_ref, g_scr, st_scr, *, tt, cs, nl):
    t = pl.program_id(1)

    @pl.when(t == 0)
    def _():
        st_scr[...] = s0_ref[...]

    xg = jnp.dot(dtga_ref[...].astype(BF16), wgate_ref[...], preferred_element_type=F32) + bgate_ref[...]
    g_scr[...] = (jnp.minimum(xg, 0.0) - jnp.log1p(jnp.exp(-jnp.abs(xg)))) * (1.0 / GLA_TAU)

    qhead = lax.broadcasted_iota(jnp.int32, (cs, V7X_LANES), 1) // GLA_DK
    vhead = lax.broadcasted_iota(jnp.int32, (cs, GLA_WIDTH), 1) // GLA_DV
    bi = lax.broadcasted_iota(jnp.int32, (GLA_WIDTH, V7X_LANES), 0) // GLA_DV
    bj = lax.broadcasted_iota(jnp.int32, (GLA_WIDTH, V7X_LANES), 1) // GLA_DK
    blockdiag = bi == bj

    def chunk(c, carry):
        r0 = pl.multiple_of(c * cs, cs)
        g = g_scr[pl.ds(r0, cs), :]
        qk = gqk_ref[pl.ds(r0, cs), :]
        q = qk[:, :V7X_LANES] * (GLA_DK ** -0.5)
        k = qk[:, V7X_LANES:]
        v = gv_ref[pl.ds(r0, cs), :]
        g_hi = g.astype(BF16)
        g_lo = (g - g_hi.astype(F32)).astype(BF16)
        f2 = jnp.dot(sums_ref[...], jnp.concatenate([g_hi, g_lo], axis=1), preferred_element_type=F32)
        e = jnp.exp(f2[:, :V7X_LANES] + f2[:, V7X_LANES:])
        att = [None] * GLA_HEADS
        for lv in range(nl + 1):
            if lv < nl:
                e_lv = e[lv * cs:(lv + 1) * cs]
                qh, kh = q * e_lv, (k * e_lv).astype(BF16)
            else:
                qh, kh = q, k.astype(BF16)
            qstack = jnp.concatenate([jnp.where(qhead == h, qh, 0.0) for h in range(GLA_HEADS)], axis=0).astype(BF16)
            out = lax.dot_general(qstack, kh, _NT, preferred_element_type=F32)
            m = masks_ref[lv]
            for h in range(GLA_HEADS):
                part = m * out[h * cs:(h + 1) * cs]
                att[h] = part if att[h] is None else att[h] + part
        e_c = e[nl * cs:(nl + 1) * cs]
        e_rev = e[(nl + 1) * cs:(nl + 2) * cs]
        qt = (q * e_c).astype(BF16)
        kt = (k * e_rev).astype(BF16)
        a_cat = jnp.concatenate([a.astype(BF16) for a in att], axis=1)
        v_bd = jnp.concatenate([jnp.where(vhead == h, v, jnp.zeros_like(v)) for h in range(GLA_HEADS)], axis=0)
        st = st_scr[...]
        o = (jnp.dot(a_cat, v_bd, preferred_element_type=F32)
             + lax.dot_general(qt, st.astype(BF16), _NT, preferred_element_type=F32))
        upd = lax.dot_general(v, kt, _TN, preferred_element_type=F32)
        st_scr[...] = st * e_c[cs - 1:cs, :] + jnp.where(blockdiag, upd, 0.0)
        ms = _dot_exact_r(o * o, ones_ref[...]) * (1.0 / GLA_DV)
        r = gr_ref[pl.ds(r0, cs), :].astype(F32)
        y_ref[pl.ds(r0, cs), :] = (o * lax.rsqrt(ms + EPS) * norm_ref[...] * _silu(r)).astype(BF16)
        return carry

    lax.fori_loop(0, tt // cs, chunk, 0, unroll=True)

    @pl.when(t == pl.num_programs(1) - 1)
    def _():
        sout_ref[...] = st_scr[...]


def _gla_call(gqk, gv, gr, dtga, s0, consts, layer, batch, seq):
    tt = min(SEQ_TILE, seq)
    cs = min(SCAN_CHUNK, seq)
    nt = seq // tt
    nl, sums_np, masks_np = _gla_level_constants(cs)
    sums = jnp.asarray(sums_np, BF16)
    masks = jnp.asarray(masks_np, F32)
    ones = jnp.asarray(np.kron(np.eye(GLA_HEADS), np.ones((GLA_DV, GLA_DV))).astype(np.float32), BF16)
    tile = lambda w: pl.BlockSpec((tt, w), lambda b, t: (b * nt + t, 0))
    per_layer = lambda shape: _resident((None,) + shape, lambda b, t: (layer, 0, 0))
    const = lambda a: _resident(a.shape, lambda b, t: (0,) * a.ndim)
    est = 4 * (tt * (256 * 4 + 256 * 2 + 256 * 2 + 128 * 4 + 256 * 2)
               + 3 * (nl + 2) * cs * 128 * 4 + 12 * cs * cs * 4) + sums.size * 2 + masks.size * 4
    return pl.pallas_call(
        functools.partial(_gla_kernel, tt=tt, cs=cs, nl=nl),
        grid=(batch, nt),
        in_specs=[tile(2 * V7X_LANES), tile(GLA_WIDTH), tile(GLA_WIDTH), tile(V7X_LANES),
                  pl.BlockSpec((None, GLA_WIDTH, V7X_LANES), lambda b, t: (b, 0, 0)),
                  per_layer((V7X_LANES, V7X_LANES)), per_layer((1, V7X_LANES)), per_layer((1, GLA_WIDTH)),
                  const(sums), const(masks), const(ones)],
        out_specs=[tile(GLA_WIDTH), pl.BlockSpec((None, GLA_WIDTH, V7X_LANES), lambda b, t: (b, 0, 0))],
        out_shape=[jax.ShapeDtypeStruct((batch * seq, GLA_WIDTH), BF16),
                   jax.ShapeDtypeStruct((batch, GLA_WIDTH, V7X_LANES), F32)],
        scratch_shapes=[pltpu.VMEM((tt, V7X_LANES), F32), pltpu.VMEM((GLA_WIDTH, V7X_LANES), F32)],
        compiler_params=pltpu.CompilerParams(
            dimension_semantics=("arbitrary", "arbitrary"), vmem_limit_bytes=_vmem_limit(est)),
        name="gla",
    )(gqk, gv, gr, dtga, s0, consts["gla_w_gate"], consts["gla_b_gate"], consts["gla_norm"], sums, masks, ones)


N_MAPS = 2 * DIFF_HEADS


def _masked_queries(q):
    group = lax.broadcasted_iota(jnp.int32, q.shape, 1) // DIFF_DK
    return jnp.concatenate([jnp.where(group == hm, q, jnp.zeros_like(q)) for hm in range(N_MAPS)], axis=0)


def _values_with_ones(v):
    left = lax.broadcasted_iota(jnp.int32, (v.shape[0], V7X_LANES), 1) < DIFF_DV
    out = []
    for pr in range(DIFF_HEADS // 2):
        vp = v[:, V7X_LANES * pr:V7X_LANES * (pr + 1)]
        out.append(jnp.where(left, vp, 1.0).astype(BF16))
        out.append(jnp.where(left, 1.0, vp).astype(BF16))
    return out


def _lane_tile(m, width):
    if width % V7X_LANES:
        return m[:, :width]
    return jnp.concatenate([m] * (width // V7X_LANES), axis=1)


def _attend(qm_all, k, vhs, m_prev, acc_prev, bias):
    tq = m_prev[0].shape[0]
    tk = k.shape[0]
    s_all = lax.dot_general(qm_all, k, _NT, preferred_element_type=F32)
    m_out, acc_out = [], []
    for h in range(DIFF_HEADS):
        ps, alphas = [], []
        for mp in range(2):
            hm = 2 * h + mp
            s = s_all[hm * tq:(hm + 1) * tq]
            if bias is not None:
                s = s + bias
            m_new = jnp.maximum(m_prev[hm], jnp.max(s, axis=-1, keepdims=True))
            m_out.append(m_new)
            alphas.append(jnp.exp2(m_prev[hm] - m_new))
            ps.append(jnp.exp2(s - _lane_tile(m_new, tk)).astype(BF16))
        pv = jnp.dot(jnp.concatenate(ps, axis=0), vhs[h], preferred_element_type=F32)
        for mp in range(2):
            acc_out.append(alphas[mp] * acc_prev[2 * h + mp] + pv[mp * tq:(mp + 1) * tq])
    return m_out, acc_out


def _diff_finalize(accs, lam_ref, norm_ref, ones_ref, lam_init):
    lq = lam_ref[...]
    s1 = jnp.sum(lq[0:1, :] * lq[1:2, :], axis=-1, keepdims=True)
    s2 = jnp.sum(lq[2:3, :] * lq[3:4, :], axis=-1, keepdims=True)
    lam = jnp.exp(s1) - jnp.exp(s2) + lam_init
    tq = accs[0].shape[0]
    left = lax.broadcasted_iota(jnp.int32, (tq, V7X_LANES), 1) < DIFF_DV
    outs = []
    for pr in range(DIFF_HEADS // 2):
        ratio = []
        for mp in range(2):
            even, odd = accs[4 * pr + mp], accs[4 * pr + 2 + mp]
            num = jnp.where(left, even, odd)
            den = pltpu.roll(jnp.where(left, odd, even), DIFF_DV, axis=1)
            ratio.append(num / den)
        o = ratio[0] - lam * ratio[1]
        ms = _dot_exact_r(o * o, ones_ref[...]) * (1.0 / DIFF_DV)
        outs.append(o * lax.rsqrt(ms + EPS) * norm_ref[...] * (1.0 - lam_init))
    return jnp.concatenate(outs, axis=1)


_BOTH_FULL = 0
_FULL_DIAG = 1
_DIAG_ONLY = 2


def _attn_prompt_kernel(qi_ref, kb_ref, kind_ref, q_ref, k_ref, v_ref, lam_ref, norm_ref, bias_ref, ones_ref, o_ref,
                        qm_scr, m_scr, acc_scr, *, tile, lam_init):
    p = pl.program_id(1)
    kind = kind_ref[p]

    @pl.when(kb_ref[p] == 0)
    def _():
        qm_scr[...] = _masked_queries(q_ref[...])
        m_scr[...] = jnp.full(m_scr.shape, -jnp.inf, F32)
        acc_scr[...] = jnp.zeros(acc_scr.shape, F32)

    def run(biases, final):
        m = [m_scr[hm] for hm in range(N_MAPS)]
        acc = [acc_scr[hm] for hm in range(N_MAPS)]
        for half, bias in enumerate(biases):
            rows = slice(half * tile, (half + 1) * tile)
            m, acc = _attend(qm_scr[...], k_ref[rows, :].astype(BF16), _values_with_ones(v_ref[rows, :]), m, acc, bias)
        if final:
            o_ref[...] = _diff_finalize(acc, lam_ref, norm_ref, ones_ref, lam_init).astype(BF16)
        else:
            for hm in range(N_MAPS):
                m_scr[hm] = m[hm]
                acc_scr[hm] = acc[hm]

    @pl.when(kind == _BOTH_FULL)
    def _():
        run([None, None], final=False)

    @pl.when(kind == _FULL_DIAG)
    def _():
        run([None, bias_ref[...]], final=True)

    @pl.when(kind == _DIAG_ONLY)
    def _():
        run([bias_ref[...]], final=True)


def _chunk_causal_bias(tile):
    c = np.arange(tile) // CHUNK
    return jnp.asarray(np.where(c[None, :] <= c[:, None], 0.0, -np.inf), F32)


def _head_pair_ones():
    return jnp.asarray(np.kron(np.eye(V7X_LANES // DIFF_DV), np.ones((DIFF_DV, DIFF_DV))), BF16)


def _attn_prompt_call(q, k, v, lam_rows, norm, layer, batch, seq, lam_init):
    tile = min(ATTN_TILE, seq // 2)
    nq = seq // tile
    nkb = nq // 2
    assert seq == nkb * 2 * tile, (seq, tile)
    steps = []
    for a in range(nq):
        steps += [(a, kb, _BOTH_FULL) for kb in range(a // 2)]
        steps.append((a, a // 2, _FULL_DIAG if a % 2 else _DIAG_ONLY))
    qi, kb, kind = (jnp.asarray([s[j] for s in steps], jnp.int32) for j in range(3))
    qspec = pl.BlockSpec((tile, DIFF_WIDTH), lambda b, p, qi, kb, kind: (b * nq + qi[p], 0))
    kspec = pl.BlockSpec((None, 2 * tile, DIFF_WIDTH), lambda b, p, qi, kb, kind: (layer, b * nkb + kb[p], 0))
    per_layer = lambda shape: pl.BlockSpec((None,) + shape, lambda b, p, qi, kb, kind: (layer, 0, 0))
    const = lambda a: pl.BlockSpec(a.shape, lambda b, p, qi, kb, kind: (0,) * a.ndim)
    bias, ones = _chunk_causal_bias(tile), _head_pair_ones()
    est = (4 * tile * DIFF_WIDTH * (2 + 4 + 4 + 2) + N_MAPS * tile * (DIFF_WIDTH * 2 + 2 * V7X_LANES * 4)
           + 5 * N_MAPS * tile * tile * 4 + 2 * tile * tile * 4)
    return pl.pallas_call(
        functools.partial(_attn_prompt_kernel, tile=tile, lam_init=lam_init),
        grid_spec=pltpu.PrefetchScalarGridSpec(
            num_scalar_prefetch=3,
            grid=(batch, len(steps)),
            in_specs=[qspec, kspec, kspec, per_layer((V7X_SUBLANES, V7X_LANES)), per_layer((1, V7X_LANES)),
                      const(bias), const(ones)],
            out_specs=qspec,
            scratch_shapes=[pltpu.VMEM((N_MAPS * tile, DIFF_WIDTH), BF16),
                            pltpu.VMEM((N_MAPS, tile, V7X_LANES), F32),
                            pltpu.VMEM((N_MAPS, tile, V7X_LANES), F32)]),
        out_shape=jax.ShapeDtypeStruct((batch * seq, DIFF_WIDTH), BF16),
        compiler_params=pltpu.CompilerParams(
            dimension_semantics=("arbitrary", "arbitrary"), vmem_limit_bytes=_vmem_limit(est)),
        name="attn_prompt",
    )(qi, kb, kind, q, k, v, lam_rows, norm, bias, ones)


def _attn_sample_kernel(q_ref, k_ref, v_ref, kp_ref, vp_ref, lam_ref, norm_ref, ones_ref, o_ref, *, lam_init):
    qm_all = _masked_queries(q_ref[...])
    tq = q_ref.shape[0]
    m = [jnp.full((tq, V7X_LANES), -jnp.inf, F32)] * N_MAPS
    acc = [jnp.zeros((tq, V7X_LANES), F32)] * N_MAPS
    m, acc = _attend(qm_all, kp_ref[...].astype(BF16), _values_with_ones(vp_ref[...]), m, acc, None)
    m, acc = _attend(qm_all, k_ref[...].astype(BF16), _values_with_ones(v_ref[...]), m, acc, None)
    o_ref[...] = _diff_finalize(acc, lam_ref, norm_ref, ones_ref, lam_init).astype(BF16)


def _attn_sample_call(q, k, v, k_past, v_past, lam_rows, norm, layer, batch, seq, lam_init):
    past = k_past.shape[2]
    qspec = pl.BlockSpec((seq, DIFF_WIDTH), lambda b: (b, 0))
    new = pl.BlockSpec((None, seq, DIFF_WIDTH), lambda b: (layer, b, 0))
    old = pl.BlockSpec((None, None, past, DIFF_WIDTH), lambda b: (layer, b, 0, 0))
    per_layer = lambda shape: pl.BlockSpec((None,) + shape, lambda b: (layer, 0, 0))
    ones = _head_pair_ones()
    est = 4 * past * DIFF_WIDTH * 4 + 3 * past * DIFF_WIDTH * 2 + 3 * N_MAPS * seq * past * 4
    return pl.pallas_call(
        functools.partial(_attn_sample_kernel, lam_init=lam_init),
        grid=(batch,),
        in_specs=[qspec, new, new, old, old, per_layer((V7X_SUBLANES, V7X_LANES)), per_layer((1, V7X_LANES)),
                  pl.BlockSpec(ones.shape, lambda b: (0, 0))],
        out_specs=qspec,
        out_shape=jax.ShapeDtypeStruct((batch * seq, DIFF_WIDTH), BF16),
        compiler_params=pltpu.CompilerParams(
            dimension_semantics=("arbitrary",), vmem_limit_bytes=_vmem_limit(est)),
        name="attn_sample",
    )(q, k, v, k_past, v_past, lam_rows, norm, ones)


def _pad_lanes(a, width):
    return jnp.pad(a, [(0, 0)] * (a.ndim - 1) + [(0, width - a.shape[-1])])


def _prepare_params(p):
    cols, start = [], 0
    for s in IN_SPLITS:
        cols.append(p["w_in"][..., start:start + s])
        start += s
    z, xbc, dt, gq, gk, gv, gr, ga, aq, ak, av = cols
    zeros = jnp.zeros(p["w_in"].shape[:2] + (_PAD_SMALL,), p["w_in"].dtype)
    w_in = jnp.concatenate([z, xbc, gq, gk, gv, gr, aq, ak, av, dt, ga, zeros], axis=-1).astype(BF16)
    row = lambda a: a[:, None, :]
    w_gate = jnp.zeros((DEPTH, V7X_LANES, V7X_LANES), F32)
    w_gate = w_gate.at[:, SSD_HEADS:SSD_HEADS + GLA_LOWRANK, :].set(p["gla_w_gate"]).astype(BF16)
    lam_rows = jnp.stack([p["diff_lambda_q1"], p["diff_lambda_k1"], p["diff_lambda_q2"], p["diff_lambda_k2"]], axis=1)
    lam_rows = jnp.pad(lam_rows, ((0, 0), (0, V7X_SUBLANES - 4), (0, V7X_LANES - DIFF_DK)))
    return dict(
        ffn1_norm=row(p["ffn1_norm"]), ffn2_norm=row(p["ffn2_norm"]), mix_norm=row(p["mix_norm"]),
        final_norm=p["final_norm"][None, :],
        ffn1=(p["ffn1_w_gate"].astype(BF16), p["ffn1_w_up"].astype(BF16), p["ffn1_w_down"].astype(BF16)),
        ffn2=(p["ffn2_w_gate"].astype(BF16), p["ffn2_w_up"].astype(BF16), p["ffn2_w_down"].astype(BF16)),
        w_in=w_in, w_out=p["w_out"].astype(BF16),
        conv_w=jnp.pad(p["ssd_conv_w"], ((0, 0), (0, V7X_SUBLANES - SSD_CONV), (0, 0))),
        conv_b=row(p["ssd_conv_b"]),
        dt_bias=row(_pad_lanes(p["ssd_dt_bias"], V7X_LANES)), a_log=row(_pad_lanes(p["ssd_a_log"], V7X_LANES)),
        d_exp=row(jnp.repeat(p["ssd_d"], SSD_HEAD_DIM, axis=-1)), ssd_norm=row(p["ssd_norm"]),
        gla_w_gate=w_gate, gla_b_gate=row(p["gla_b_gate"]), gla_norm=row(jnp.tile(p["gla_norm"], (1, GLA_HEADS))),
        lam_rows=lam_rows, diff_norm=row(jnp.tile(p["diff_norm"], (1, V7X_LANES // DIFF_DV))),
    )


def _ssd_state_to_pairs(s):
    b = s.shape[0]
    st = jnp.swapaxes(s, -1, -2).reshape(b, SSD_HEADS // 2, 2, SSD_STATE, SSD_HEAD_DIM)
    eye = jnp.eye(2, dtype=s.dtype)
    bd = st[:, :, :, :, None, :] * eye[None, None, :, None, :, None]
    return bd.reshape(b, SSD_HEADS // 2, 2 * SSD_STATE, 2 * SSD_HEAD_DIM)


def _ssd_state_from_pairs(bd):
    b = bd.shape[0]
    x = bd.reshape(b, SSD_HEADS // 2, 2, SSD_STATE, 2, SSD_HEAD_DIM)
    st = jnp.stack([x[:, :, 0, :, 0, :], x[:, :, 1, :, 1, :]], axis=2)
    return jnp.swapaxes(st.reshape(b, SSD_HEADS, SSD_STATE, SSD_HEAD_DIM), -1, -2)


def _gla_state_to_blocks(s):
    b = s.shape[0]
    st = jnp.swapaxes(s, -1, -2)
    eye = jnp.eye(GLA_HEADS, dtype=s.dtype)
    bd = st[:, :, :, None, :] * eye[None, :, None, :, None]
    return bd.reshape(b, GLA_HEADS * GLA_DV, GLA_HEADS * GLA_DK)


def _gla_state_from_blocks(bd):
    b = bd.shape[0]
    x = bd.reshape(b, GLA_HEADS, GLA_DV, GLA_HEADS, GLA_DK)
    st = jnp.stack([x[:, h, :, h, :] for h in range(GLA_HEADS)], axis=1)
    return jnp.swapaxes(st, -1, -2)


def _trunk(x, conv_st, ssd_st, gla_st, k_cache, v_cache, pp):
    batch, seq, d = x.shape
    prompt = k_cache is None
    x = x.reshape(batch * seq, d)
    convs, ssds, glas = [], [], []
    kv_bufs = None
    hdr = V7X_SUBLANES
    for l in range(DEPTH):
        lam_init = 0.8 - 0.6 * float(np.exp(-0.3 * l))
        x = _ffn_call(x, l, pp["ffn1_norm"], *pp["ffn1"])
        z, xbc, gqk, gv, gr, aq, ak, av, dtga = _inproj_call(x, l, pp["mix_norm"], pp["w_in"], kv_bufs)
        kv_bufs = (ak, av)
        if prompt:
            conv_prev = jnp.zeros((batch, hdr, SSD_CONV_CH), F32)
            ssd0 = jnp.zeros((batch, SSD_HEADS // 2, V7X_LANES, V7X_LANES), F32)
            gla0 = jnp.zeros((batch, GLA_WIDTH, V7X_LANES), F32)
        else:
            conv_prev = jnp.pad(conv_st[l], ((0, 0), (hdr - (SSD_CONV - 1), 0), (0, 0)))
            ssd0 = _ssd_state_to_pairs(ssd_st[l].astype(F32))
            gla0 = _gla_state_to_blocks(gla_st[l].astype(F32))
        y_ssd, conv_out, ssd_out = _ssd_call(xbc, z, dtga, conv_prev, ssd0, pp, l, batch, seq)
        y_gla, gla_out = _gla_call(gqk, gv, gr, dtga, gla0, pp, l, batch, seq)
        if prompt:
            y_diff = _attn_prompt_call(aq, ak, av, pp["lam_rows"], pp["diff_norm"], l, batch, seq, lam_init)
        else:
            past = k_cache.shape[2]
            y_diff = _attn_sample_call(aq, ak, av, k_cache.reshape(DEPTH, batch, past, DIFF_WIDTH),
                                       v_cache.reshape(DEPTH, batch, past, DIFF_WIDTH),
                                       pp["lam_rows"], pp["diff_norm"], l, batch, seq, lam_init)
        last = l == DEPTH - 1
        x = _ffn_call(x, l, pp["ffn2_norm"], *pp["ffn2"], mix=(y_ssd, y_gla, y_diff, pp["w_out"]),
                      final_norm=pp["final_norm"] if last else None)
        convs.append(conv_out[:, hdr - (SSD_CONV - 1):, :])
        ssds.append(_ssd_state_from_pairs(ssd_out))
        glas.append(_gla_state_from_blocks(gla_out))
    k_all, v_all = kv_bufs
    return (x.reshape(batch, seq, d), jnp.stack(convs), jnp.stack(ssds), jnp.stack(glas),
            k_all.reshape(DEPTH, batch, seq, DIFF_HEADS, 2 * DIFF_DK),
            v_all.reshape(DEPTH, batch, seq, DIFF_HEADS, DIFF_DV))


def kernel(x_prompt, x_sample, state_ssd_conv, state_ssd, state_gla, cache_diff_k, cache_diff_v, ffn1_norm, ffn1_w_gate, ffn1_w_up, ffn1_w_down, mix_norm, w_in, w_out, ssd_conv_w, ssd_conv_b, ssd_dt_bias, ssd_a_log, ssd_d, ssd_norm, gla_w_gate, gla_b_gate, gla_norm, diff_lambda_q1, diff_lambda_k1, diff_lambda_q2, diff_lambda_k2, diff_norm, ffn2_norm, ffn2_w_gate, ffn2_w_up, ffn2_w_down, final_norm):
    pp = _prepare_params(dict(
        ffn1_norm=ffn1_norm, ffn1_w_gate=ffn1_w_gate, ffn1_w_up=ffn1_w_up, ffn1_w_down=ffn1_w_down,
        mix_norm=mix_norm, w_in=w_in, w_out=w_out, ssd_conv_w=ssd_conv_w, ssd_conv_b=ssd_conv_b,
        ssd_dt_bias=ssd_dt_bias, ssd_a_log=ssd_a_log, ssd_d=ssd_d, ssd_norm=ssd_norm,
        gla_w_gate=gla_w_gate, gla_b_gate=gla_b_gate, gla_norm=gla_norm,
        diff_lambda_q1=diff_lambda_q1, diff_lambda_k1=diff_lambda_k1, diff_lambda_q2=diff_lambda_q2,
        diff_lambda_k2=diff_lambda_k2, diff_norm=diff_norm,
        ffn2_norm=ffn2_norm, ffn2_w_gate=ffn2_w_gate, ffn2_w_up=ffn2_w_up, ffn2_w_down=ffn2_w_down,
        final_norm=final_norm))
    y_p, conv_p, ssd_p, gla_p, k_p, v_p = _trunk(x_prompt, None, None, None, None, None, pp)
    y_s, conv_s, ssd_s, gla_s, k_s, v_s = _trunk(
        x_sample, state_ssd_conv, state_ssd, state_gla, cache_diff_k, cache_diff_v, pp)
    return (y_p, y_s, conv_p, conv_s, ssd_p, ssd_s, gla_p, gla_s, k_p, k_s, v_p, v_s)
```

```python
import functools

import jax
import jax.numpy as jnp
import numpy as np
from jax import lax
from jax.experimental import pallas as pl
from jax.experimental.pallas import tpu as pltpu

F32 = jnp.float32
BF16 = jnp.bfloat16

EPS = 1e-5
DEPTH = 4
D_MODEL = 1024
D_FF = 2816
CHUNK = 64

SSD_WIDTH = 512
SSD_HEADS = 8
SSD_HEAD_DIM = 64
SSD_GROUPS = 2
SSD_STATE = 64
SSD_CONV = 4
SSD_CONV_CH = SSD_WIDTH + 2 * SSD_GROUPS * SSD_STATE

GLA_WIDTH = 256
GLA_HEADS = 4
GLA_DK = 32
GLA_DV = 64
GLA_LOWRANK = 16
GLA_TAU = 16.0

DIFF_WIDTH = 256
DIFF_HEADS = 4
DIFF_DK = 32
DIFF_DV = 64

IN_SPLITS = (SSD_WIDTH, SSD_CONV_CH, SSD_HEADS,
             GLA_HEADS * GLA_DK, GLA_HEADS * GLA_DK, GLA_WIDTH, GLA_WIDTH, GLA_LOWRANK,
             DIFF_HEADS * 2 * DIFF_DK, DIFF_HEADS * 2 * DIFF_DK, DIFF_WIDTH)

V7X_LANES = 128
V7X_SUBLANES = 8
V7X_VMEM_BYTES = 64 * 1024 * 1024

_PAD_SMALL = V7X_LANES - SSD_HEADS - GLA_LOWRANK
IN_PADDED = sum(IN_SPLITS) + _PAD_SMALL
O_Z, O_XBC, O_GQ, O_GV, O_GR, O_AQ, O_DTGA, O_AK, O_AV = 0, 512, 1280, 1536, 1792, 2048, 2304, 2432, 2688

TM = 512
SEQ_TILE = 512
SCAN_CHUNK = 128
ATTN_TILE = 512

_NT = (((1,), (1,)), ((), ()))
_TN = (((0,), (0,)), ((), ()))


def _vmem_limit(nbytes):
    return int(min(max(nbytes, 16 * 1024 * 1024), V7X_VMEM_BYTES - 6 * 1024 * 1024))


def _rmsnorm(x, g):
    ms = jnp.mean(x * x, axis=-1, keepdims=True)
    return x * lax.rsqrt(ms + EPS) * g


def _silu(x):
    return x * jax.nn.sigmoid(x)


def _split3(x):
    hi = x.astype(BF16)
    r = x - hi.astype(F32)
    mid = r.astype(BF16)
    lo = (r - mid.astype(F32)).astype(BF16)
    return hi, mid, lo


def _dot_exact_l(a, x):
    return sum(jnp.dot(a, p, preferred_element_type=F32) for p in _split3(x))


def _dot_exact_r(x, a):
    return sum(jnp.dot(p, a, preferred_element_type=F32) for p in _split3(x))


def _dot_exact_nt(a, x):
    return sum(lax.dot_general(a, p, _NT, preferred_element_type=F32) for p in _split3(x))


def _resident(block_shape, index_map):
    return pl.BlockSpec(block_shape, index_map, pipeline_mode=pl.Buffered(1))


V7X_MXU_DIM = 256
_FF_CUT = (D_FF // V7X_MXU_DIM + 1) // 2 * V7X_MXU_DIM
FF_BLOCKS = ((0, _FF_CUT), (_FF_CUT, D_FF))


def _ffn_kernel(*refs, with_mix, with_final):
    it = iter(refs)
    x_ref = next(it)
    if with_mix:
        y1_ref, y2_ref, y3_ref, wo_ref = next(it), next(it), next(it), next(it)
    g_ref, wg_ref, wu_ref, wd_ref = next(it), next(it), next(it), next(it)
    if with_final:
        fn_ref = next(it)
    o_ref = next(it)

    x = x_ref[...]
    if with_mix:
        y = jnp.concatenate([y1_ref[...], y2_ref[...], y3_ref[...]], axis=-1)
        x = x + jnp.dot(y, wo_ref[...], preferred_element_type=F32)
    h = _rmsnorm(x, g_ref[...]).astype(BF16)
    acc = None
    for lo, hi in FF_BLOCKS:
        gate = jnp.dot(h, wg_ref[:, lo:hi], preferred_element_type=F32)
        up = jnp.dot(h, wu_ref[:, lo:hi], preferred_element_type=F32)
        act = (_silu(gate) * up).astype(BF16)
        part = jnp.dot(act, wd_ref[lo:hi, :], preferred_element_type=F32)
        acc = part if acc is None else acc + part
    x = x + 0.5 * acc
    if with_final:
        x = _rmsnorm(x, fn_ref[...])
    o_ref[...] = x


def _ffn_call(x, layer, norm, wg, wu, wd, mix=None, final_norm=None):
    t, d = x.shape
    tm = min(TM, t)
    tile = lambda w: pl.BlockSpec((tm, w), lambda i: (i, 0))
    per_layer = lambda shape: _resident((None,) + shape, lambda i: (layer, 0, 0))
    args, specs = [x], [tile(d)]
    if mix is not None:
        y1, y2, y3, wo = mix
        args += [y1, y2, y3, wo]
        specs += [tile(y1.shape[1]), tile(y2.shape[1]), tile(y3.shape[1]), per_layer((d, d))]
    args += [norm, wg, wu, wd]
    specs += [per_layer((1, d)), per_layer((d, D_FF)), per_layer((d, D_FF)), per_layer((D_FF, d))]
    if final_norm is not None:
        args.append(final_norm)
        specs.append(_resident((1, d), lambda i: (0, 0)))
    weights = 2 * (3 * d * D_FF + d * d)
    tiles = 2 * (2 * tm * d * 4 + tm * d * 2)
    temps = tm * (3 * d * 4 + _FF_CUT * (4 + 4 + 2))
    return pl.pallas_call(
        functools.partial(_ffn_kernel, with_mix=mix is not None, with_final=final_norm is not None),
        grid=(t // tm,),
        in_specs=specs,
        out_specs=tile(d),
        out_shape=jax.ShapeDtypeStruct((t, d), F32),
        compiler_params=pltpu.CompilerParams(
            dimension_semantics=("arbitrary",), vmem_limit_bytes=_vmem_limit(weights + tiles + 2 * temps)),
        name="ffn",
    )(*args)


def _inproj_kernel(x_ref, g_ref, w_ref, *refs, kv_transposed):
    wt_ref = refs[0]
    z_ref, xbc_ref, gqk_ref, gv_ref, gr_ref, aq_ref, dtga_ref, ak_ref, av_ref = refs[-9:]
    h = _rmsnorm(x_ref[...], g_ref[...]).astype(BF16)
    if kv_transposed:
        u = jnp.dot(h, w_ref[:, :O_AK], preferred_element_type=F32)
        kv_t = lax.dot_general(wt_ref[...], h, _NT, preferred_element_type=F32)
        ak_ref[...] = kv_t[:DIFF_WIDTH]
        av_ref[...] = kv_t[DIFF_WIDTH:]
    else:
        u = jnp.dot(h, w_ref[...], preferred_element_type=F32)
        ak_ref[...] = u[:, O_AK:O_AV]
        av_ref[...] = u[:, O_AV:IN_PADDED]
    z_ref[...] = u[:, O_Z:O_XBC].astype(BF16)
    xbc_ref[...] = u[:, O_XBC:O_GQ]
    gqk_ref[...] = u[:, O_GQ:O_GV]
    gv_ref[...] = u[:, O_GV:O_GR].astype(BF16)
    gr_ref[...] = u[:, O_GR:O_AQ].astype(BF16)
    aq_ref[...] = (u[:, O_AQ:O_DTGA] * (DIFF_DK ** -0.5 * np.log2(np.e))).astype(BF16)
    dtga_ref[...] = u[:, O_DTGA:O_AK]


def _inproj_call(x, layer, norm, w, w_kv_t, kv_bufs, batch, seq, kv_transposed):
    t, d = x.shape
    widths = [(O_XBC - O_Z, BF16), (O_GQ - O_XBC, F32), (O_GV - O_GQ, F32), (O_GR - O_GV, BF16),
              (O_AQ - O_GR, BF16), (O_DTGA - O_AQ, BF16), (O_AK - O_DTGA, F32)]
    tm = min(TM, t)
    tile = lambda w_: pl.BlockSpec((tm, w_), lambda i: (i, 0))
    per_layer = lambda shape: _resident((None,) + shape, lambda i: (layer, 0, 0))
    out_specs = [tile(w_) for w_, _ in widths]
    out_shape = [jax.ShapeDtypeStruct((t, w_), dt) for w_, dt in widths]
    if kv_transposed:
        nt = seq // tm
        out_specs += [pl.BlockSpec((None, None, DIFF_WIDTH, tm), lambda i: (layer, i // nt, 0, i % nt))] * 2
        out_shape += [jax.ShapeDtypeStruct((DEPTH, batch, DIFF_WIDTH, seq), F32)] * 2
    else:
        out_specs += [pl.BlockSpec((None, tm, DIFF_WIDTH), lambda i: (layer, i, 0))] * 2
        out_shape += [jax.ShapeDtypeStruct((DEPTH, t, DIFF_WIDTH), F32)] * 2
    args = [x, norm, w, w_kv_t]
    in_specs = [tile(d), per_layer((1, d)), per_layer((d, IN_PADDED)), per_layer((2 * DIFF_WIDTH, d))]
    aliases = {}
    if kv_bufs is not None:
        for j, buf in enumerate(kv_bufs):
            aliases[len(args)] = len(widths) + j
            args.append(buf)
            in_specs.append(pl.BlockSpec(memory_space=pl.ANY))
    est = 2 * d * (IN_PADDED + 2 * DIFF_WIDTH) + 2 * tm * d * 4 + 3 * tm * IN_PADDED * 4
    return pl.pallas_call(
        functools.partial(_inproj_kernel, kv_transposed=kv_transposed),
        grid=(t // tm,),
        in_specs=in_specs,
        out_specs=out_specs,
        out_shape=out_shape,
        input_output_aliases=aliases,
        compiler_params=pltpu.CompilerParams(
            dimension_semantics=("arbitrary",), vmem_limit_bytes=_vmem_limit(est)),
        name="inproj",
    )(*args)


def _ssd_kernel(xbc_ref, z_ref, dtga_ref, cprev_ref, s0_ref, cw_ref, cb_ref, dtb_ref, alog_ref, dexp_ref,
                norm_ref, ltri_ref, ident_ref, expand_ref,
                y_ref, cout_ref, sout_ref,
                xp_scr, conv_scr, a_scr, dt_scr, s_scr, *, tt, cs):
    t = pl.program_id(1)
    hdr = V7X_SUBLANES

    @pl.when(t == 0)
    def _():
        xp_scr[0:hdr, :] = cprev_ref[...]
        s_scr[...] = s0_ref[...]

    xp_scr[hdr:hdr + tt, :] = xbc_ref[...]
    conv = cb_ref[...]
    for i in range(SSD_CONV):
        off = hdr - (SSD_CONV - 1) + i
        conv = conv + xp_scr[off:off + tt, :] * cw_ref[i:i + 1, :]
    conv_scr[...] = _silu(conv)
    xp_scr[0:hdr, :] = xp_scr[tt:tt + hdr, :]

    lane = lax.broadcasted_iota(jnp.int32, (tt, V7X_LANES), 1)
    x_dt = dtga_ref[...] + dtb_ref[...]
    dt = jnp.maximum(x_dt, 0.0) + jnp.log1p(jnp.exp(-jnp.abs(x_dt)))
    dt = jnp.where(lane < SSD_HEADS, dt, 0.0)
    dt_scr[...] = dt
    a_scr[...] = dt * (-jnp.exp(alog_ref[...]))

    left = lax.broadcasted_iota(jnp.int32, (cs, V7X_LANES), 1) < SSD_STATE
    ri = lax.broadcasted_iota(jnp.int32, (cs, cs), 0)
    ci = lax.broadcasted_iota(jnp.int32, (cs, cs), 1)
    causal = ci <= ri
    bi = lax.broadcasted_iota(jnp.int32, (V7X_LANES, V7X_LANES), 0) // SSD_STATE
    bj = lax.broadcasted_iota(jnp.int32, (V7X_LANES, V7X_LANES), 1) // SSD_HEAD_DIM
    blockdiag = bi == bj

    def chunk(c, carry):
        r0 = pl.multiple_of(c * cs, cs)
        conv_c = conv_scr[pl.ds(r0, cs), :]
        xs = conv_c[:, :SSD_WIDTH]
        bm = conv_c[:, SSD_WIDTH:SSD_WIDTH + V7X_LANES]
        cm = conv_c[:, SSD_WIDTH + V7X_LANES:]
        a_c = a_scr[pl.ds(r0, cs), :]
        dt_c = dt_scr[pl.ds(r0, cs), :]
        cum = _dot_exact_l(ltri_ref[...], a_c)
        cum_t = _dot_exact_nt(ident_ref[...], cum)
        last = cum[cs - 1:cs, :]
        e_cum = jnp.exp(cum)
        e_rev = jnp.exp(last - cum)
        small = jnp.concatenate([dt_c, e_cum, e_rev], axis=0).astype(BF16)
        wide = jnp.dot(small, expand_ref[...], preferred_element_type=F32)
        dt_x, ecum_x, erev_x = wide[0:cs], wide[cs:2 * cs], wide[2 * cs:3 * cs]
        elast_x = _dot_exact_r(jnp.broadcast_to(e_cum[cs - 1:cs, :], (V7X_SUBLANES, V7X_LANES)),
                               expand_ref[...])[0:1, :]
        xd = xs * dt_x
        bm_sw = pltpu.roll(bm, SSD_STATE, axis=1)
        cm_sw = pltpu.roll(cm, SSD_STATE, axis=1)
        ys = []
        for g in range(SSD_GROUPS):
            if g == 0:
                bdup, cdup, cmask = jnp.where(left, bm, bm_sw), jnp.where(left, cm, cm_sw), jnp.where(left, cm, 0.0)
            else:
                bdup, cdup, cmask = jnp.where(left, bm_sw, bm), jnp.where(left, cm_sw, cm), jnp.where(left, 0.0, cm)
            gmat = lax.dot_general(cmask.astype(BF16), bm.astype(BF16), _NT, preferred_element_type=F32)
            for pr in (2 * g, 2 * g + 1):
                sl = slice(V7X_LANES * pr, V7X_LANES * (pr + 1))
                xd_p = xd[:, sl]
                top = jnp.where(left, xd_p, 0.0).astype(BF16)
                bot = jnp.where(left, 0.0, xd_p).astype(BF16)
                scores = []
                for hd in range(2):
                    h = 2 * pr + hd
                    arg = cum[:, h:h + 1] - cum_t[h:h + 1, :]
                    decay = jnp.where(causal, jnp.exp(jnp.minimum(arg, 0.0)), 0.0)
                    scores.append((gmat * decay).astype(BF16))
                ce = (cdup * ecum_x[:, sl]).astype(BF16)
                s_pair = s_scr[pr]
                lhs = jnp.concatenate([scores[0], scores[1], ce], axis=1)
                rhs = jnp.concatenate([top, bot, s_pair.astype(BF16)], axis=0)
                ys.append(jnp.dot(lhs, rhs, preferred_element_type=F32))
                bt = (bdup * erev_x[:, sl]).astype(BF16)
                upd = lax.dot_general(bt, xd_p.astype(BF16), _TN, preferred_element_type=F32)
                s_scr[pr] = s_pair * elast_x[:, sl] + jnp.where(blockdiag, upd, 0.0)
        y = jnp.concatenate(ys, axis=1) + dexp_ref[...] * xs
        y = y * _silu(z_ref[pl.ds(r0, cs), :].astype(F32))
        y_ref[pl.ds(r0, cs), :] = _rmsnorm(y, norm_ref[...]).astype(BF16)
        return carry

    lax.fori_loop(0, tt // cs, chunk, 0, unroll=True)

    @pl.when(t == pl.num_programs(1) - 1)
    def _():
        cout_ref[...] = xp_scr[0:hdr, :]
        sout_ref[...] = s_scr[...]


def _ssd_call(xbc, z, dtga, conv_prev, s0, consts, layer, batch, seq):
    tt = min(SEQ_TILE, seq)
    cs = min(SCAN_CHUNK, seq)
    nt = seq // tt
    hdr = V7X_SUBLANES
    tile = lambda w: pl.BlockSpec((tt, w), lambda b, t: (b * nt + t, 0))
    per_layer = lambda shape: _resident((None,) + shape, lambda b, t: (layer, 0, 0))
    const = lambda a: _resident(a.shape, lambda b, t: (0,) * a.ndim)
    npair = SSD_HEADS // 2
    ltri = jnp.asarray(np.tril(np.ones((cs, cs), np.float32)), BF16)
    ident = jnp.asarray(np.eye(V7X_LANES, dtype=np.float32), BF16)
    ex = np.zeros((V7X_LANES, SSD_WIDTH), np.float32)
    for h in range(SSD_HEADS):
        ex[h, h * SSD_HEAD_DIM:(h + 1) * SSD_HEAD_DIM] = 1.0
    expand = jnp.asarray(ex, BF16)
    est = 4 * (tt * (SSD_CONV_CH * 4 + SSD_WIDTH * 2 + 128 * 4 + SSD_WIDTH * 2)
               + (tt + hdr) * SSD_CONV_CH * 4 + tt * SSD_CONV_CH * 4 + 16 * cs * cs * 4)
    return pl.pallas_call(
        functools.partial(_ssd_kernel, tt=tt, cs=cs),
        grid=(batch, nt),
        in_specs=[tile(SSD_CONV_CH), tile(SSD_WIDTH), tile(V7X_LANES),
                  pl.BlockSpec((None, hdr, SSD_CONV_CH), lambda b, t: (b, 0, 0)),
                  pl.BlockSpec((None, npair, V7X_LANES, V7X_LANES), lambda b, t: (b, 0, 0, 0)),
                  per_layer((hdr, SSD_CONV_CH)), per_layer((1, SSD_CONV_CH)), per_layer((1, V7X_LANES)),
                  per_layer((1, V7X_LANES)), per_layer((1, SSD_WIDTH)), per_layer((1, SSD_WIDTH)),
                  const(ltri), const(ident), const(expand)],
        out_specs=[tile(SSD_WIDTH),
                   pl.BlockSpec((None, hdr, SSD_CONV_CH), lambda b, t: (b, 0, 0)),
                   pl.BlockSpec((None, npair, V7X_LANES, V7X_LANES), lambda b, t: (b, 0, 0, 0))],
        out_shape=[jax.ShapeDtypeStruct((batch * seq, SSD_WIDTH), BF16),
                   jax.ShapeDtypeStruct((batch, hdr, SSD_CONV_CH), F32),
                   jax.ShapeDtypeStruct((batch, npair, V7X_LANES, V7X_LANES), F32)],
        scratch_shapes=[pltpu.VMEM((tt + hdr, SSD_CONV_CH), F32), pltpu.VMEM((tt, SSD_CONV_CH), F32),
                        pltpu.VMEM((tt, V7X_LANES), F32), pltpu.VMEM((tt, V7X_LANES), F32),
                        pltpu.VMEM((npair, V7X_LANES, V7X_LANES), F32)],
        compiler_params=pltpu.CompilerParams(
            dimension_semantics=("arbitrary", "arbitrary"), vmem_limit_bytes=_vmem_limit(est)),
        name="ssd",
    )(xbc, z, dtga, conv_prev, s0, consts["conv_w"], consts["conv_b"], consts["dt_bias"], consts["a_log"],
      consts["d_exp"], consts["ssd_norm"], ltri, ident, expand)


def _gla_level_constants(cs):
    nl = int(np.log2(cs))
    idx = np.arange(cs)
    sums = np.zeros(((nl + 2) * cs, cs), np.float32)
    masks = np.zeros((nl + 1, cs, cs), np.float32)
    for lv in range(nl):
        s = 1 << lv
        blk = idx // (2 * s)
        right = (idx // s) % 2 == 1
        m = blk * 2 * s + s - 1
        for i in range(cs):
            if right[i]:
                sums[lv * cs + i, m[i] + 1:i + 1] = 1.0
            else:
                sums[lv * cs + i, i + 1:m[i] + 1] = 1.0
        masks[lv] = (blk[:, None] == blk[None, :]) & right[:, None] & ~right[None, :]
    masks[nl] = np.eye(cs)
    sums[nl * cs:(nl + 1) * cs] = np.tril(np.ones((cs, cs)))
    sums[(nl + 1) * cs:] = np.triu(np.ones((cs, cs)), 1)
    return nl, sums, masks


def _gla_kernel(gqk_ref, gv_ref, gr_ref, dtga_ref, s0_ref, wgate_ref, bgate_ref, norm_ref, sums_ref, masks_ref,
                ones_ref, y_ref, sout_ref, g_scr, st_scr, *, tt, cs, nl):
    t = pl.program_id(1)

    @pl.when(t == 0)
    def _():
        st_scr[...] = s0_ref[...]

    xg = jnp.dot(dtga_ref[...].astype(BF16), wgate_ref[...], preferred_element_type=F32) + bgate_ref[...]
    g_scr[...] = (jnp.minimum(xg, 0.0) - jnp.log1p(jnp.exp(-jnp.abs(xg)))) * (1.0 / GLA_TAU)

    qhead = lax.broadcasted_iota(jnp.int32, (cs, V7X_LANES), 1) // GLA_DK
    vhead = lax.broadcasted_iota(jnp.int32, (cs, GLA_WIDTH), 1) // GLA_DV
    bi = lax.broadcasted_iota(jnp.int32, (GLA_WIDTH, V7X_LANES), 0) // GLA_DV
    bj = lax.broadcasted_iota(jnp.int32, (GLA_WIDTH, V7X_LANES), 1) // GLA_DK
    blockdiag = bi == bj

    def chunk(c, carry):
        r0 = pl.multiple_of(c * cs, cs)
        g = g_scr[pl.ds(r0, cs), :]
        qk = gqk_ref[pl.ds(r0, cs), :]
        q = qk[:, :V7X_LANES] * (GLA_DK ** -0.5)
        k = qk[:, V7X_LANES:]
        v = gv_ref[pl.ds(r0, cs), :]
        g_hi = g.astype(BF16)
        g_lo = (g - g_hi.astype(F32)).astype(BF16)
        f2 = jnp.dot(sums_ref[...], jnp.concatenate([g_hi, g_lo], axis=1), preferred_element_type=F32)
        e = jnp.exp(f2[:, :V7X_LANES] + f2[:, V7X_LANES:])
        att = [None] * GLA_HEADS
        for lv in range(nl + 1):
            if lv < nl:
                e_lv = e[lv * cs:(lv + 1) * cs]
                qh, kh = q * e_lv, (k * e_lv).astype(BF16)
            else:
                qh, kh = q, k.astype(BF16)
            qstack = jnp.concatenate([jnp.where(qhead == h, qh, 0.0) for h in range(GLA_HEADS)], axis=0).astype(BF16)
            out = lax.dot_general(qstack, kh, _NT, preferred_element_type=F32)
            m = masks_ref[lv]
            for h in range(GLA_HEADS):
                part = m * out[h * cs:(h + 1) * cs]
                att[h] = part if att[h] is None else att[h] + part
        e_c = e[nl * cs:(nl + 1) * cs]
        e_rev = e[(nl + 1) * cs:(nl + 2) * cs]
        qt = (q * e_c).astype(BF16)
        kt = (k * e_rev).astype(BF16)
        a_cat = jnp.concatenate([a.astype(BF16) for a in att], axis=1)
        v_bd = jnp.concatenate([jnp.where(vhead == h, v, jnp.zeros_like(v)) for h in range(GLA_HEADS)], axis=0)
        st = st_scr[...]
        o = (jnp.dot(a_cat, v_bd, preferred_element_type=F32)
             + lax.dot_general(qt, st.astype(BF16), _NT, preferred_element_type=F32))
        upd = lax.dot_general(v, kt, _TN, preferred_element_type=F32)
        st_scr[...] = st * e_c[cs - 1:cs, :] + jnp.where(blockdiag, upd, 0.0)
        ms = _dot_exact_r(o * o, ones_ref[...]) * (1.0 / GLA_DV)
        r = gr_ref[pl.ds(r0, cs), :].astype(F32)
        y_ref[pl.ds(r0, cs), :] = (o * lax.rsqrt(ms + EPS) * norm_ref[...] * _silu(r)).astype(BF16)
        return carry

    lax.fori_loop(0, tt // cs, chunk, 0, unroll=True)

    @pl.when(t == pl.num_programs(1) - 1)
    def _():
        sout_ref[...] = st_scr[...]


def _gla_call(gqk, gv, gr, dtga, s0, consts, layer, batch, seq):
    tt = min(SEQ_TILE, seq)
    cs = min(SCAN_CHUNK, seq)
    nt = seq // tt
    nl, sums_np, masks_np = _gla_level_constants(cs)
    sums = jnp.asarray(sums_np, BF16)
    masks = jnp.asarray(masks_np, F32)
    ones = jnp.asarray(np.kron(np.eye(GLA_HEADS), np.ones((GLA_DV, GLA_DV))).astype(np.float32), BF16)
    tile = lambda w: pl.BlockSpec((tt, w), lambda b, t: (b * nt + t, 0))
    per_layer = lambda shape: _resident((None,) + shape, lambda b, t: (layer, 0, 0))
    const = lambda a: _resident(a.shape, lambda b, t: (0,) * a.ndim)
    est = 4 * (tt * (256 * 4 + 256 * 2 + 256 * 2 + 128 * 4 + 256 * 2)
               + 3 * (nl + 2) * cs * 128 * 4 + 12 * cs * cs * 4) + sums.size * 2 + masks.size * 4
    return pl.pallas_call(
        functools.partial(_gla_kernel, tt=tt, cs=cs, nl=nl),
        grid=(batch, nt),
        in_specs=[tile(2 * V7X_LANES), tile(GLA_WIDTH), tile(GLA_WIDTH), tile(V7X_LANES),
                  pl.BlockSpec((None, GLA_WIDTH, V7X_LANES), lambda b, t: (b, 0, 0)),
                  per_layer((V7X_LANES, V7X_LANES)), per_layer((1, V7X_LANES)), per_layer((1, GLA_WIDTH)),
                  const(sums), const(masks), const(ones)],
        out_specs=[tile(GLA_WIDTH), pl.BlockSpec((None, GLA_WIDTH, V7X_LANES), lambda b, t: (b, 0, 0))],
        out_shape=[jax.ShapeDtypeStruct((batch * seq, GLA_WIDTH), BF16),
                   jax.ShapeDtypeStruct((batch, GLA_WIDTH, V7X_LANES), F32)],
        scratch_shapes=[pltpu.VMEM((tt, V7X_LANES), F32), pltpu.VMEM((GLA_WIDTH, V7X_LANES), F32)],
        compiler_params=pltpu.CompilerParams(
            dimension_semantics=("arbitrary", "arbitrary"), vmem_limit_bytes=_vmem_limit(est)),
        name="gla",
    )(gqk, gv, gr, dtga, s0, consts["gla_w_gate"], consts["gla_b_gate"], consts["gla_norm"], sums, masks, ones)


N_MAPS = 2 * DIFF_HEADS


def _masked_queries(q):
    group = lax.broadcasted_iota(jnp.int32, q.shape, 1) // DIFF_DK
    return jnp.concatenate([jnp.where(group == hm, q, jnp.zeros_like(q)) for hm in range(N_MAPS)], axis=0)


def _values_with_ones(v):
    left = lax.broadcasted_iota(jnp.int32, (v.shape[0], V7X_LANES), 1) < DIFF_DV
    out = []
    for pr in range(DIFF_HEADS // 2):
        vp = v[:, V7X_LANES * pr:V7X_LANES * (pr + 1)]
        out.append(jnp.where(left, vp, 1.0).astype(BF16))
        out.append(jnp.where(left, 1.0, vp).astype(BF16))
    return out


def _values_with_ones_t(v_t):
    ones = jnp.ones((DIFF_DV, v_t.shape[1]), BF16)
    out = []
    for h in range(DIFF_HEADS):
        vh = v_t[h * DIFF_DV:(h + 1) * DIFF_DV, :].astype(BF16)
        out.append(jnp.concatenate([ones, vh] if h % 2 else [vh, ones], axis=0))
    return out


def _lane_tile(m, width):
    if width % V7X_LANES:
        return m[:, :width]
    return jnp.concatenate([m] * (width // V7X_LANES), axis=1)


def _attend(qm_all, k, vhs, m_prev, acc_prev, bias, transposed):
    tq = m_prev[0].shape[0]
    if transposed:
        s_all = jnp.dot(qm_all, k, preferred_element_type=F32)
    else:
        s_all = lax.dot_general(qm_all, k, _NT, preferred_element_type=F32)
    tk = s_all.shape[1]
    m_out, acc_out = [], []
    for h in range(DIFF_HEADS):
        ps, alphas = [], []
        for mp in range(2):
            hm = 2 * h + mp
            s = s_all[hm * tq:(hm + 1) * tq]
            if bias is not None:
                s = s + bias
            m_new = jnp.maximum(m_prev[hm], jnp.max(s, axis=-1, keepdims=True))
            m_out.append(m_new)
            alphas.append(jnp.exp2(m_prev[hm] - m_new))
            ps.append(jnp.exp2(s - _lane_tile(m_new, tk)).astype(BF16))
        p_both = jnp.concatenate(ps, axis=0)
        if transposed:
            pv = lax.dot_general(p_both, vhs[h], _NT, preferred_element_type=F32)
        else:
            pv = jnp.dot(p_both, vhs[h], preferred_element_type=F32)
        for mp in range(2):
            acc_out.append(alphas[mp] * acc_prev[2 * h + mp] + pv[mp * tq:(mp + 1) * tq])
    return m_out, acc_out


def _diff_finalize(accs, lam_ref, norm_ref, ones_ref, lam_init):
    lq = lam_ref[...]
    s1 = jnp.sum(lq[0:1, :] * lq[1:2, :], axis=-1, keepdims=True)
    s2 = jnp.sum(lq[2:3, :] * lq[3:4, :], axis=-1, keepdims=True)
    lam = jnp.exp(s1) - jnp.exp(s2) + lam_init
    tq = accs[0].shape[0]
    left = lax.broadcasted_iota(jnp.int32, (tq, V7X_LANES), 1) < DIFF_DV
    outs = []
    for pr in range(DIFF_HEADS // 2):
        ratio = []
        for mp in range(2):
            even, odd = accs[4 * pr + mp], accs[4 * pr + 2 + mp]
            num = jnp.where(left, even, odd)
            den = pltpu.roll(jnp.where(left, odd, even), DIFF_DV, axis=1)
            ratio.append(num / den)
        o = ratio[0] - lam * ratio[1]
        ms = _dot_exact_r(o * o, ones_ref[...]) * (1.0 / DIFF_DV)
        outs.append(o * lax.rsqrt(ms + EPS) * norm_ref[...] * (1.0 - lam_init))
    return jnp.concatenate(outs, axis=1)


_BOTH_FULL = 0
_FULL_DIAG = 1
_DIAG_ONLY = 2


def _attn_prompt_kernel(qi_ref, kb_ref, kind_ref, q_ref, k_ref, v_ref, lam_ref, norm_ref, bias_ref, ones_ref, o_ref,
                        qm_scr, m_scr, acc_scr, *, tile, lam_init):
    p = pl.program_id(1)
    kind = kind_ref[p]

    @pl.when(kb_ref[p] == 0)
    def _():
        qm_scr[...] = _masked_queries(q_ref[...])
        m_scr[...] = jnp.full(m_scr.shape, -jnp.inf, F32)
        acc_scr[...] = jnp.zeros(acc_scr.shape, F32)

    def run(biases, final):
        m = [m_scr[hm] for hm in range(N_MAPS)]
        acc = [acc_scr[hm] for hm in range(N_MAPS)]
        for half, bias in enumerate(biases):
            keys = slice(half * tile, (half + 1) * tile)
            m, acc = _attend(qm_scr[...], k_ref[:, keys].astype(BF16), _values_with_ones_t(v_ref[:, keys]), m, acc,
                             bias, transposed=True)
        if final:
            o_ref[...] = _diff_finalize(acc, lam_ref, norm_ref, ones_ref, lam_init).astype(BF16)
        else:
            for hm in range(N_MAPS):
                m_scr[hm] = m[hm]
                acc_scr[hm] = acc[hm]

    @pl.when(kind == _BOTH_FULL)
    def _():
        run([None, None], final=False)

    @pl.when(kind == _FULL_DIAG)
    def _():
        run([None, bias_ref[...]], final=True)

    @pl.when(kind == _DIAG_ONLY)
    def _():
        run([bias_ref[...]], final=True)


def _chunk_causal_bias(tile):
    c = np.arange(tile) // CHUNK
    return jnp.asarray(np.where(c[None, :] <= c[:, None], 0.0, -np.inf), F32)


def _head_pair_ones():
    return jnp.asarray(np.kron(np.eye(V7X_LANES // DIFF_DV), np.ones((DIFF_DV, DIFF_DV))), BF16)


def _attn_prompt_call(q, k, v, lam_rows, norm, layer, batch, seq, lam_init):
    tile = min(ATTN_TILE, seq // 2)
    nq = seq // tile
    nkb = nq // 2
    assert seq == nkb * 2 * tile, (seq, tile)
    steps = []
    for a in range(nq):
        steps += [(a, kb, _BOTH_FULL) for kb in range(a // 2)]
        steps.append((a, a // 2, _FULL_DIAG if a % 2 else _DIAG_ONLY))
    qi, kb, kind = (jnp.asarray([s[j] for s in steps], jnp.int32) for j in range(3))
    qspec = pl.BlockSpec((tile, DIFF_WIDTH), lambda b, p, qi, kb, kind: (b * nq + qi[p], 0))
    kspec = pl.BlockSpec((None, None, DIFF_WIDTH, 2 * tile), lambda b, p, qi, kb, kind: (layer, b, 0, kb[p]))
    per_layer = lambda shape: pl.BlockSpec((None,) + shape, lambda b, p, qi, kb, kind: (layer, 0, 0))
    const = lambda a: pl.BlockSpec(a.shape, lambda b, p, qi, kb, kind: (0,) * a.ndim)
    bias, ones = _chunk_causal_bias(tile), _head_pair_ones()
    est = (4 * tile * DIFF_WIDTH * (2 + 4 + 4 + 2) + N_MAPS * tile * (DIFF_WIDTH * 2 + 2 * V7X_LANES * 4)
           + 5 * N_MAPS * tile * tile * 4 + 2 * tile * tile * 4)
    return pl.pallas_call(
        functools.partial(_attn_prompt_kernel, tile=tile, lam_init=lam_init),
        grid_spec=pltpu.PrefetchScalarGridSpec(
            num_scalar_prefetch=3,
            grid=(batch, len(steps)),
            in_specs=[qspec, kspec, kspec, per_layer((V7X_SUBLANES, V7X_LANES)), per_layer((1, V7X_LANES)),
                      const(bias), const(ones)],
            out_specs=qspec,
            scratch_shapes=[pltpu.VMEM((N_MAPS * tile, DIFF_WIDTH), BF16),
                            pltpu.VMEM((N_MAPS, tile, V7X_LANES), F32),
                            pltpu.VMEM((N_MAPS, tile, V7X_LANES), F32)]),
        out_shape=jax.ShapeDtypeStruct((batch * seq, DIFF_WIDTH), BF16),
        compiler_params=pltpu.CompilerParams(
            dimension_semantics=("arbitrary", "arbitrary"), vmem_limit_bytes=_vmem_limit(est)),
        name="attn_prompt",
    )(qi, kb, kind, q, k, v, lam_rows, norm, bias, ones)


def _attn_sample_kernel(q_ref, k_ref, v_ref, kp_ref, vp_ref, lam_ref, norm_ref, ones_ref, o_ref, *, lam_init):
    qm_all = _masked_queries(q_ref[...])
    tq = q_ref.shape[0]
    m = [jnp.full((tq, V7X_LANES), -jnp.inf, F32)] * N_MAPS
    acc = [jnp.zeros((tq, V7X_LANES), F32)] * N_MAPS
    m, acc = _attend(qm_all, kp_ref[...].astype(BF16), _values_with_ones_t(vp_ref[...]), m, acc, None,
                     transposed=True)
    m, acc = _attend(qm_all, k_ref[...].astype(BF16), _values_with_ones(v_ref[...]), m, acc, None,
                     transposed=False)
    o_ref[...] = _diff_finalize(acc, lam_ref, norm_ref, ones_ref, lam_init).astype(BF16)


def _attn_sample_call(q, k, v, k_past, v_past, lam_rows, norm, layer, batch, seq, lam_init):
    past = k_past.shape[3]
    qspec = pl.BlockSpec((seq, DIFF_WIDTH), lambda b: (b, 0))
    new = pl.BlockSpec((None, seq, DIFF_WIDTH), lambda b: (layer, b, 0))
    old = pl.BlockSpec((None, None, DIFF_WIDTH, past), lambda b: (layer, b, 0, 0))
    per_layer = lambda shape: pl.BlockSpec((None,) + shape, lambda b: (layer, 0, 0))
    ones = _head_pair_ones()
    est = 4 * past * DIFF_WIDTH * 4 + 3 * past * DIFF_WIDTH * 2 + 3 * N_MAPS * seq * past * 4
    return pl.pallas_call(
        functools.partial(_attn_sample_kernel, lam_init=lam_init),
        grid=(batch,),
        in_specs=[qspec, new, new, old, old, per_layer((V7X_SUBLANES, V7X_LANES)), per_layer((1, V7X_LANES)),
                  pl.BlockSpec(ones.shape, lambda b: (0, 0))],
        out_specs=qspec,
        out_shape=jax.ShapeDtypeStruct((batch * seq, DIFF_WIDTH), BF16),
        compiler_params=pltpu.CompilerParams(
            dimension_semantics=("arbitrary",), vmem_limit_bytes=_vmem_limit(est)),
        name="attn_sample",
    )(q, k, v, k_past, v_past, lam_rows, norm, ones)


def _pad_lanes(a, width):
    return jnp.pad(a, [(0, 0)] * (a.ndim - 1) + [(0, width - a.shape[-1])])


def _prepare_params(p):
    cols, start = [], 0
    for s in IN_SPLITS:
        cols.append(p["w_in"][..., start:start + s])
        start += s
    z, xbc, dt, gq, gk, gv, gr, ga, aq, ak, av = cols
    zeros = jnp.zeros(p["w_in"].shape[:2] + (_PAD_SMALL,), p["w_in"].dtype)
    w_in = jnp.concatenate([z, xbc, gq, gk, gv, gr, aq, dt, ga, zeros, ak, av], axis=-1).astype(BF16)
    w_kv_t = jnp.swapaxes(jnp.concatenate([ak, av], axis=-1), -1, -2).astype(BF16)
    row = lambda a: a[:, None, :]
    w_gate = jnp.zeros((DEPTH, V7X_LANES, V7X_LANES), F32)
    w_gate = w_gate.at[:, SSD_HEADS:SSD_HEADS + GLA_LOWRANK, :].set(p["gla_w_gate"]).astype(BF16)
    lam_rows = jnp.stack([p["diff_lambda_q1"], p["diff_lambda_k1"], p["diff_lambda_q2"], p["diff_lambda_k2"]], axis=1)
    lam_rows = jnp.pad(lam_rows, ((0, 0), (0, V7X_SUBLANES - 4), (0, V7X_LANES - DIFF_DK)))
    return dict(
        ffn1_norm=row(p["ffn1_norm"]), ffn2_norm=row(p["ffn2_norm"]), mix_norm=row(p["mix_norm"]),
        final_norm=p["final_norm"][None, :],
        ffn1=(p["ffn1_w_gate"].astype(BF16), p["ffn1_w_up"].astype(BF16), p["ffn1_w_down"].astype(BF16)),
        ffn2=(p["ffn2_w_gate"].astype(BF16), p["ffn2_w_up"].astype(BF16), p["ffn2_w_down"].astype(BF16)),
        w_in=w_in, w_kv_t=w_kv_t, w_out=p["w_out"].astype(BF16),
        conv_w=jnp.pad(p["ssd_conv_w"], ((0, 0), (0, V7X_SUBLANES - SSD_CONV), (0, 0))),
        conv_b=row(p["ssd_conv_b"]),
        dt_bias=row(_pad_lanes(p["ssd_dt_bias"], V7X_LANES)), a_log=row(_pad_lanes(p["ssd_a_log"], V7X_LANES)),
        d_exp=row(jnp.repeat(p["ssd_d"], SSD_HEAD_DIM, axis=-1)), ssd_norm=row(p["ssd_norm"]),
        gla_w_gate=w_gate, gla_b_gate=row(p["gla_b_gate"]), gla_norm=row(jnp.tile(p["gla_norm"], (1, GLA_HEADS))),
        lam_rows=lam_rows, diff_norm=row(jnp.tile(p["diff_norm"], (1, V7X_LANES // DIFF_DV))),
    )


def _ssd_state_to_pairs(s):
    b = s.shape[0]
    st = jnp.swapaxes(s, -1, -2).reshape(b, SSD_HEADS // 2, 2, SSD_STATE, SSD_HEAD_DIM)
    eye = jnp.eye(2, dtype=s.dtype)
    bd = st[:, :, :, :, None, :] * eye[None, None, :, None, :, None]
    return bd.reshape(b, SSD_HEADS // 2, 2 * SSD_STATE, 2 * SSD_HEAD_DIM)


def _ssd_state_from_pairs(bd):
    b = bd.shape[0]
    x = bd.reshape(b, SSD_HEADS // 2, 2, SSD_STATE, 2, SSD_HEAD_DIM)
    st = jnp.stack([x[:, :, 0, :, 0, :], x[:, :, 1, :, 1, :]], axis=2)
    return jnp.swapaxes(st.reshape(b, SSD_HEADS, SSD_STATE, SSD_HEAD_DIM), -1, -2)


def _gla_state_to_blocks(s):
    b = s.shape[0]
    st = jnp.swapaxes(s, -1, -2)
    eye = jnp.eye(GLA_HEADS, dtype=s.dtype)
    bd = st[:, :, :, None, :] * eye[None, :, None, :, None]
    return bd.reshape(b, GLA_HEADS * GLA_DV, GLA_HEADS * GLA_DK)


def _gla_state_from_blocks(bd):
    b = bd.shape[0]
    x = bd.reshape(b, GLA_HEADS, GLA_DV, GLA_HEADS, GLA_DK)
    st = jnp.stack([x[:, h, :, h, :] for h in range(GLA_HEADS)], axis=1)
    return jnp.swapaxes(st, -1, -2)


def _trunk(x, conv_st, ssd_st, gla_st, k_cache, v_cache, pp):
    batch, seq, d = x.shape
    prompt = k_cache is None
    x = x.reshape(batch * seq, d)
    convs, ssds, glas = [], [], []
    kv_bufs = None
    hdr = V7X_SUBLANES
    for l in range(DEPTH):
        lam_init = 0.8 - 0.6 * float(np.exp(-0.3 * l))
        x = _ffn_call(x, l, pp["ffn1_norm"], *pp["ffn1"])
        z, xbc, gqk, gv, gr, aq, dtga, ak, av = _inproj_call(
            x, l, pp["mix_norm"], pp["w_in"], pp["w_kv_t"], kv_bufs, batch, seq, kv_transposed=prompt)
        kv_bufs = (ak, av)
        if prompt:
            conv_prev = jnp.zeros((batch, hdr, SSD_CONV_CH), F32)
            ssd0 = jnp.zeros((batch, SSD_HEADS // 2, V7X_LANES, V7X_LANES), F32)
            gla0 = jnp.zeros((batch, GLA_WIDTH, V7X_LANES), F32)
        else:
            conv_prev = jnp.pad(conv_st[l], ((0, 0), (hdr - (SSD_CONV - 1), 0), (0, 0)))
            ssd0 = _ssd_state_to_pairs(ssd_st[l].astype(F32))
            gla0 = _gla_state_to_blocks(gla_st[l].astype(F32))
        y_ssd, conv_out, ssd_out = _ssd_call(xbc, z, dtga, conv_prev, ssd0, pp, l, batch, seq)
        y_gla, gla_out = _gla_call(gqk, gv, gr, dtga, gla0, pp, l, batch, seq)
        if prompt:
            y_diff = _attn_prompt_call(aq, ak, av, pp["lam_rows"], pp["diff_norm"], l, batch, seq, lam_init)
        else:
            feature_major = lambda c: jnp.transpose(c, (0, 1, 3, 4, 2)).reshape(DEPTH, batch, DIFF_WIDTH, c.shape[2])
            y_diff = _attn_sample_call(aq, ak, av, feature_major(k_cache), feature_major(v_cache),
                                       pp["lam_rows"], pp["diff_norm"], l, batch, seq, lam_init)
        last = l == DEPTH - 1
        x = _ffn_call(x, l, pp["ffn2_norm"], *pp["ffn2"], mix=(y_ssd, y_gla, y_diff, pp["w_out"]),
                      final_norm=pp["final_norm"] if last else None)
        convs.append(conv_out[:, hdr - (SSD_CONV - 1):, :])
        ssds.append(_ssd_state_from_pairs(ssd_out))
        glas.append(_gla_state_from_blocks(gla_out))
    if prompt:
        k_out, v_out = (jnp.transpose(a.reshape(DEPTH, batch, DIFF_HEADS, DIFF_DV, seq), (0, 1, 4, 2, 3))
                        for a in kv_bufs)
    else:
        k_out, v_out = (a.reshape(DEPTH, batch, seq, DIFF_HEADS, DIFF_DV) for a in kv_bufs)
    return (x.reshape(batch, seq, d), jnp.stack(convs), jnp.stack(ssds), jnp.stack(glas), k_out, v_out)


def kernel(x_prompt, x_sample, state_ssd_conv, state_ssd, state_gla, cache_diff_k, cache_diff_v, ffn1_norm, ffn1_w_gate, ffn1_w_up, ffn1_w_down, mix_norm, w_in, w_out, ssd_conv_w, ssd_conv_b, ssd_dt_bias, ssd_a_log, ssd_d, ssd_norm, gla_w_gate, gla_b_gate, gla_norm, diff_lambda_q1, diff_lambda_k1, diff_lambda_q2, diff_lambda_k2, diff_norm, ffn2_norm, ffn2_w_gate, ffn2_w_up, ffn2_w_down, final_norm):
    pp = _prepare_params(dict(
        ffn1_norm=ffn1_norm, ffn1_w_gate=ffn1_w_gate, ffn1_w_up=ffn1_w_up, ffn1_w_down=ffn1_w_down,
        mix_norm=mix_norm, w_in=w_in, w_out=w_out, ssd_conv_w=ssd_conv_w, ssd_conv_b=ssd_conv_b,
        ssd_dt_bias=ssd_dt_bias, ssd_a_log=ssd_a_log, ssd_d=ssd_d, ssd_norm=ssd_norm,
        gla_w_gate=gla_w_gate, gla_b_gate=gla_b_gate, gla_norm=gla_norm,
        diff_lambda_q1=diff_lambda_q1, diff_lambda_k1=diff_lambda_k1, diff_lambda_q2=diff_lambda_q2,
        diff_lambda_k2=diff_lambda_k2, diff_norm=diff_norm,
        ffn2_norm=ffn2_norm, ffn2_w_gate=ffn2_w_gate, ffn2_w_up=ffn2_w_up, ffn2_w_down=ffn2_w_down,
        final_norm=final_norm))
    y_p, conv_p, ssd_p, gla_p, k_p, v_p = _trunk(x_prompt, None, None, None, None, None, pp)
    y_s, conv_s, ssd_s, gla_s, k_s, v_s = _trunk(
        x_sample, state_ssd_conv, state_ssd, state_gla, cache_diff_k, cache_diff_v, pp)
    return (y_p, y_s, conv_p, conv_s, ssd_p, ssd_s, gla_p, gla_s, k_p, k_s, v_p, v_s)
```

```python
import functools

import jax
import jax.numpy as jnp
import numpy as np
from jax import lax
from jax.experimental import pallas as pl
from jax.experimental.pallas import tpu as pltpu

F32 = jnp.float32
BF16 = jnp.bfloat16

EPS = 1e-5
DEPTH = 4
D_MODEL = 1024
D_FF = 2816
CHUNK = 64

SSD_WIDTH = 512
SSD_HEADS = 8
SSD_HEAD_DIM = 64
SSD_GROUPS = 2
SSD_STATE = 64
SSD_CONV = 4
SSD_CONV_CH = SSD_WIDTH + 2 * SSD_GROUPS * SSD_STATE

GLA_WIDTH = 256
GLA_HEADS = 4
GLA_DK = 32
GLA_DV = 64
GLA_LOWRANK = 16
GLA_TAU = 16.0

DIFF_WIDTH = 256
DIFF_HEADS = 4
DIFF_DK = 32
DIFF_DV = 64

IN_SPLITS = (SSD_WIDTH, SSD_CONV_CH, SSD_HEADS,
             GLA_HEADS * GLA_DK, GLA_HEADS * GLA_DK, GLA_WIDTH, GLA_WIDTH, GLA_LOWRANK,
             DIFF_HEADS * 2 * DIFF_DK, DIFF_HEADS * 2 * DIFF_DK, DIFF_WIDTH)

V7X_LANES = 128
V7X_SUBLANES = 8
V7X_VMEM_BYTES = 64 * 1024 * 1024

_PAD_SMALL = V7X_LANES - SSD_HEADS - GLA_LOWRANK
IN_PADDED = sum(IN_SPLITS) + _PAD_SMALL
O_Z, O_XBC, O_GQ, O_GV, O_GR, O_AQ, O_DTGA, O_AK, O_AV = 0, 512, 1280, 1536, 1792, 2048, 2304, 2432, 2688

TM = 512
SEQ_TILE = 1024
SCAN_CHUNK = 128
ATTN_TILE = 512

_NT = (((1,), (1,)), ((), ()))
_TN = (((0,), (0,)), ((), ()))


def _vmem_limit(nbytes):
    return int(min(max(nbytes, 16 * 1024 * 1024), V7X_VMEM_BYTES - 6 * 1024 * 1024))


def _rmsnorm(x, g):
    ms = jnp.mean(x * x, axis=-1, keepdims=True)
    return x * lax.rsqrt(ms + EPS) * g


def _silu(x):
    half = 0.5 * x
    return half + half * jnp.tanh(half)


def _split3(x):
    hi = x.astype(BF16)
    r = x - hi.astype(F32)
    mid = r.astype(BF16)
    lo = (r - mid.astype(F32)).astype(BF16)
    return hi, mid, lo


def _dot_exact_l(a, x):
    return sum(jnp.dot(a, p, preferred_element_type=F32) for p in _split3(x))


def _dot_exact_r(x, a):
    return sum(jnp.dot(p, a, preferred_element_type=F32) for p in _split3(x))


def _dot_exact_nt(a, x):
    return sum(lax.dot_general(a, p, _NT, preferred_element_type=F32) for p in _split3(x))


def _resident(block_shape, index_map):
    return pl.BlockSpec(block_shape, index_map, pipeline_mode=pl.Buffered(1))


V7X_MXU_DIM = 256
_FF_CUT = (D_FF // V7X_MXU_DIM + 1) // 2 * V7X_MXU_DIM
FF_BLOCKS = ((0, _FF_CUT), (_FF_CUT, D_FF))


def _ffn_kernel(*refs, with_mix, with_final):
    it = iter(refs)
    x_ref = next(it)
    if with_mix:
        y1_ref, y2_ref, y3_ref, wo_ref = next(it), next(it), next(it), next(it)
    g_ref, wg_ref, wu_ref, wd_ref = next(it), next(it), next(it), next(it)
    if with_final:
        fn_ref = next(it)
    o_ref = next(it)

    x = x_ref[...]
    if with_mix:
        y = jnp.concatenate([y1_ref[...], y2_ref[...], y3_ref[...]], axis=-1)
        x = x + jnp.dot(y, wo_ref[...], preferred_element_type=F32)
    h = _rmsnorm(x, g_ref[...]).astype(BF16)
    acc = None
    for lo, hi in FF_BLOCKS:
        gate = jnp.dot(h, wg_ref[:, lo:hi], preferred_element_type=F32)
        up = jnp.dot(h, wu_ref[:, lo:hi], preferred_element_type=F32)
        act = (_silu(gate) * up).astype(BF16)
        part = jnp.dot(act, wd_ref[lo:hi, :], preferred_element_type=F32)
        acc = part if acc is None else acc + part
    x = x + 0.5 * acc
    if with_final:
        x = _rmsnorm(x, fn_ref[...])
    o_ref[...] = x


def _ffn_call(x, layer, norm, wg, wu, wd, mix=None, final_norm=None):
    t, d = x.shape
    tm = min(TM, t)
    tile = lambda w: pl.BlockSpec((tm, w), lambda i: (i, 0))
    per_layer = lambda shape: _resident((None,) + shape, lambda i: (layer, 0, 0))
    args, specs = [x], [tile(d)]
    if mix is not None:
        y1, y2, y3, wo = mix
        args += [y1, y2, y3, wo]
        specs += [tile(y1.shape[1]), tile(y2.shape[1]), tile(y3.shape[1]), per_layer((d, d))]
    args += [norm, wg, wu, wd]
    specs += [per_layer((1, d)), per_layer((d, D_FF)), per_layer((d, D_FF)), per_layer((D_FF, d))]
    if final_norm is not None:
        args.append(final_norm)
        specs.append(_resident((1, d), lambda i: (0, 0)))
    weights = 2 * (3 * d * D_FF + d * d)
    tiles = 2 * (2 * tm * d * 4 + tm * d * 2)
    temps = tm * (3 * d * 4 + _FF_CUT * (4 + 4 + 2))
    return pl.pallas_call(
        functools.partial(_ffn_kernel, with_mix=mix is not None, with_final=final_norm is not None),
        grid=(t // tm,),
        in_specs=specs,
        out_specs=tile(d),
        out_shape=jax.ShapeDtypeStruct((t, d), F32),
        compiler_params=pltpu.CompilerParams(
            dimension_semantics=("arbitrary",), vmem_limit_bytes=_vmem_limit(weights + tiles + 2 * temps)),
        name="ffn",
    )(*args)


def _inproj_kernel(x_ref, g_ref, w_ref, *refs, kv_transposed):
    wt_ref = refs[0]
    z_ref, xbc_ref, gqk_ref, gv_ref, gr_ref, aq_ref, dtga_ref, ak_ref, av_ref = refs[-9:]
    h = _rmsnorm(x_ref[...], g_ref[...]).astype(BF16)
    if kv_transposed:
        u = jnp.dot(h, w_ref[:, :O_AK], preferred_element_type=F32)
        kv_t = lax.dot_general(wt_ref[...], h, _NT, preferred_element_type=F32)
        ak_ref[...] = kv_t[:DIFF_WIDTH]
        av_ref[...] = kv_t[DIFF_WIDTH:]
    else:
        u = jnp.dot(h, w_ref[...], preferred_element_type=F32)
        ak_ref[...] = u[:, O_AK:O_AV]
        av_ref[...] = u[:, O_AV:IN_PADDED]
    z_ref[...] = u[:, O_Z:O_XBC].astype(BF16)
    xbc_ref[...] = u[:, O_XBC:O_GQ]
    gqk_ref[...] = u[:, O_GQ:O_GV]
    gv_ref[...] = u[:, O_GV:O_GR].astype(BF16)
    gr_ref[...] = u[:, O_GR:O_AQ].astype(BF16)
    aq_ref[...] = (u[:, O_AQ:O_DTGA] * (DIFF_DK ** -0.5 * np.log2(np.e))).astype(BF16)
    dtga_ref[...] = u[:, O_DTGA:O_AK]


def _inproj_call(x, layer, norm, w, w_kv_t, kv_bufs, batch, seq, kv_transposed):
    t, d = x.shape
    widths = [(O_XBC - O_Z, BF16), (O_GQ - O_XBC, F32), (O_GV - O_GQ, F32), (O_GR - O_GV, BF16),
              (O_AQ - O_GR, BF16), (O_DTGA - O_AQ, BF16), (O_AK - O_DTGA, F32)]
    tm = min(TM, t)
    tile = lambda w_: pl.BlockSpec((tm, w_), lambda i: (i, 0))
    per_layer = lambda shape: _resident((None,) + shape, lambda i: (layer, 0, 0))
    out_specs = [tile(w_) for w_, _ in widths]
    out_shape = [jax.ShapeDtypeStruct((t, w_), dt) for w_, dt in widths]
    if kv_transposed:
        nt = seq // tm
        out_specs += [pl.BlockSpec((None, None, DIFF_WIDTH, tm), lambda i: (layer, i // nt, 0, i % nt))] * 2
        out_shape += [jax.ShapeDtypeStruct((DEPTH, batch, DIFF_WIDTH, seq), F32)] * 2
    else:
        out_specs += [pl.BlockSpec((None, tm, DIFF_WIDTH), lambda i: (layer, i, 0))] * 2
        out_shape += [jax.ShapeDtypeStruct((DEPTH, t, DIFF_WIDTH), F32)] * 2
    args = [x, norm, w, w_kv_t]
    in_specs = [tile(d), per_layer((1, d)), per_layer((d, IN_PADDED)), per_layer((2 * DIFF_WIDTH, d))]
    aliases = {}
    if kv_bufs is not None:
        for j, buf in enumerate(kv_bufs):
            aliases[len(args)] = len(widths) + j
            args.append(buf)
            in_specs.append(pl.BlockSpec(memory_space=pl.ANY))
    est = 2 * d * (IN_PADDED + 2 * DIFF_WIDTH) + 2 * tm * d * 4 + 3 * tm * IN_PADDED * 4
    return pl.pallas_call(
        functools.partial(_inproj_kernel, kv_transposed=kv_transposed),
        grid=(t // tm,),
        in_specs=in_specs,
        out_specs=out_specs,
        out_shape=out_shape,
        input_output_aliases=aliases,
        compiler_params=pltpu.CompilerParams(
            dimension_semantics=("arbitrary",), vmem_limit_bytes=_vmem_limit(est)),
        name="inproj",
    )(*args)


def _ssd_kernel(xbc_ref, z_ref, dtga_ref, cprev_ref, s0_ref, cw_ref, cb_ref, dtb_ref, alog_ref, dexp_ref,
                norm_ref, ltri_ref, ident_ref, expand_ref, dup_ref, fold_ref,
                y_ref, cout_ref, sout_ref,
                xp_scr, conv_scr, a_scr, dt_scr, s_scr, *, tt, cs):
    t = pl.program_id(1)
    hdr = V7X_SUBLANES
    bi = lax.broadcasted_iota(jnp.int32, (V7X_LANES, V7X_LANES), 0) // SSD_HEAD_DIM
    bj = lax.broadcasted_iota(jnp.int32, (V7X_LANES, V7X_LANES), 1) // SSD_STATE
    blockdiag = bi == bj

    @pl.when(t == 0)
    def _():
        xp_scr[0:hdr, :] = cprev_ref[...]
        for pr in range(SSD_HEADS // 2):
            both = jnp.concatenate([s0_ref[2 * pr], s0_ref[2 * pr + 1]], axis=0)
            s_scr[pr] = jnp.where(blockdiag, _dot_exact_r(both, dup_ref[...]), 0.0)

    xp_scr[hdr:hdr + tt, :] = xbc_ref[...]
    x3 = xp_scr[...].reshape((tt + hdr) // hdr, hdr, SSD_CONV_CH)
    sub = lax.broadcasted_iota(jnp.int32, (1, hdr, SSD_CONV_CH), 1)
    tap = lambda i: cw_ref[i:i + 1, :].reshape(1, 1, SSD_CONV_CH)
    conv = cb_ref[...].reshape(1, 1, SSD_CONV_CH) + x3[1:] * tap(SSD_CONV - 1)
    for d in range(1, SSD_CONV):
        rot = pltpu.roll(x3, d, axis=1)
        conv = conv + jnp.where(sub < d, rot[:-1], rot[1:]) * tap(SSD_CONV - 1 - d)
    conv_scr[...] = _silu(conv).reshape(tt, SSD_CONV_CH)
    xp_scr[0:hdr, :] = xp_scr[tt:tt + hdr, :]

    lane = lax.broadcasted_iota(jnp.int32, (tt, V7X_LANES), 1)
    x_dt = dtga_ref[...] + dtb_ref[...]
    dt = jnp.maximum(x_dt, 0.0) + jnp.log1p(jnp.exp(-jnp.abs(x_dt)))
    dt = jnp.where(lane < SSD_HEADS, dt, 0.0)
    dt_scr[...] = dt
    a_scr[...] = dt * (-jnp.exp(alog_ref[...]))

    left = lax.broadcasted_iota(jnp.int32, (cs, V7X_LANES), 1) < SSD_STATE
    ri = lax.broadcasted_iota(jnp.int32, (cs, cs), 0)
    ci = lax.broadcasted_iota(jnp.int32, (cs, cs), 1)
    causal = ci <= ri
    def chunk(c, carry):
        r0 = pl.multiple_of(c * cs, cs)
        conv_c = conv_scr[pl.ds(r0, cs), :]
        xs = conv_c[:, :SSD_WIDTH]
        bm = conv_c[:, SSD_WIDTH:SSD_WIDTH + V7X_LANES]
        cm = conv_c[:, SSD_WIDTH + V7X_LANES:]
        a_c = a_scr[pl.ds(r0, cs), :]
        dt_c = dt_scr[pl.ds(r0, cs), :]
        cum = _dot_exact_l(ltri_ref[...], a_c)
        cum_t = _dot_exact_nt(ident_ref[...], cum)
        last = cum[cs - 1:cs, :]
        e_cum = jnp.exp(cum)
        e_rev = jnp.exp(last - cum)
        small = jnp.concatenate([dt_c, e_cum, e_rev], axis=0).astype(BF16)
        wide = jnp.dot(small, expand_ref[...], preferred_element_type=F32)
        dt_x, ecum_x, erev_x = wide[0:cs], wide[cs:2 * cs], wide[2 * cs:3 * cs]
        elast_x = _dot_exact_r(jnp.broadcast_to(e_cum[cs - 1:cs, :], (V7X_SUBLANES, V7X_LANES)),
                               expand_ref[...])[0:1, :]
        xd = xs * dt_x
        bm_sw = pltpu.roll(bm, SSD_STATE, axis=1)
        cm_sw = pltpu.roll(cm, SSD_STATE, axis=1)
        ys = []
        for g in range(SSD_GROUPS):
            if g == 0:
                bdup, cdup, cmask = jnp.where(left, bm, bm_sw), jnp.where(left, cm, cm_sw), jnp.where(left, cm, 0.0)
            else:
                bdup, cdup, cmask = jnp.where(left, bm_sw, bm), jnp.where(left, cm_sw, cm), jnp.where(left, 0.0, cm)
            gmat = lax.dot_general(cmask.astype(BF16), bm.astype(BF16), _NT, preferred_element_type=F32)
            for pr in (2 * g, 2 * g + 1):
                sl = slice(V7X_LANES * pr, V7X_LANES * (pr + 1))
                xd_p = xd[:, sl]
                top = jnp.where(left, xd_p, 0.0).astype(BF16)
                bot = jnp.where(left, 0.0, xd_p).astype(BF16)
                scores = []
                for hd in range(2):
                    h = 2 * pr + hd
                    arg = cum[:, h:h + 1] - cum_t[h:h + 1, :]
                    decay = jnp.where(causal, jnp.exp(jnp.minimum(arg, 0.0)), 0.0)
                    scores.append((gmat * decay).astype(BF16))
                ce = (cdup * ecum_x[:, sl]).astype(BF16)
                s_pair = s_scr[pr]
                ys.append(jnp.dot(jnp.concatenate(scores, axis=1), jnp.concatenate([top, bot], axis=0),
                                  preferred_element_type=F32)
                          + lax.dot_general(ce, s_pair.astype(BF16), _NT, preferred_element_type=F32))
                bt = (bdup * erev_x[:, sl]).astype(BF16)
                upd = lax.dot_general(xd_p.astype(BF16), bt, _TN, preferred_element_type=F32)
                s_scr[pr] = s_pair * elast_x[:, sl] + jnp.where(blockdiag, upd, 0.0)
        y = jnp.concatenate(ys, axis=1) + dexp_ref[...] * xs
        y = y * _silu(z_ref[pl.ds(r0, cs), :].astype(F32))
        y_ref[pl.ds(r0, cs), :] = _rmsnorm(y, norm_ref[...]).astype(BF16)
        return carry

    lax.fori_loop(0, tt // cs, chunk, 0, unroll=True)

    @pl.when(t == pl.num_programs(1) - 1)
    def _():
        cout_ref[...] = xp_scr[0:hdr, :]
        for pr in range(SSD_HEADS // 2):
            both = _dot_exact_r(s_scr[pr], fold_ref[...])
            sout_ref[2 * pr] = both[:SSD_HEAD_DIM]
            sout_ref[2 * pr + 1] = both[SSD_HEAD_DIM:]


def _ssd_call(xbc, z, dtga, conv_prev, s0, consts, layer, batch, seq):
    tt = min(SEQ_TILE, seq)
    cs = min(SCAN_CHUNK, seq)
    nt = seq // tt
    hdr = V7X_SUBLANES
    tile = lambda w: pl.BlockSpec((tt, w), lambda b, t: (b * nt + t, 0))
    per_layer = lambda shape: _resident((None,) + shape, lambda b, t: (layer, 0, 0))
    const = lambda a: _resident(a.shape, lambda b, t: (0,) * a.ndim)
    npair = SSD_HEADS // 2
    ltri = jnp.asarray(np.tril(np.ones((cs, cs), np.float32)), BF16)
    ident = jnp.asarray(np.eye(V7X_LANES, dtype=np.float32), BF16)
    ex = np.zeros((V7X_LANES, SSD_WIDTH), np.float32)
    for h in range(SSD_HEADS):
        ex[h, h * SSD_HEAD_DIM:(h + 1) * SSD_HEAD_DIM] = 1.0
    expand = jnp.asarray(ex, BF16)
    eye = np.eye(SSD_STATE, dtype=np.float32)
    dup = jnp.asarray(np.concatenate([eye, eye], axis=1), BF16)
    fold = jnp.asarray(np.concatenate([eye, eye], axis=0), BF16)
    state_spec = pl.BlockSpec((None, SSD_HEADS, SSD_HEAD_DIM, SSD_STATE), lambda b, t: (b, 0, 0, 0))
    est = 4 * (tt * (SSD_CONV_CH * 4 + SSD_WIDTH * 2 + 128 * 4 + SSD_WIDTH * 2)
               + (tt + hdr) * SSD_CONV_CH * 4 + tt * SSD_CONV_CH * 4 + 16 * cs * cs * 4)
    return pl.pallas_call(
        functools.partial(_ssd_kernel, tt=tt, cs=cs),
        grid=(batch, nt),
        in_specs=[tile(SSD_CONV_CH), tile(SSD_WIDTH), tile(V7X_LANES),
                  pl.BlockSpec((None, hdr, SSD_CONV_CH), lambda b, t: (b, 0, 0)),
                  state_spec,
                  per_layer((hdr, SSD_CONV_CH)), per_layer((1, SSD_CONV_CH)), per_layer((1, V7X_LANES)),
                  per_layer((1, V7X_LANES)), per_layer((1, SSD_WIDTH)), per_layer((1, SSD_WIDTH)),
                  const(ltri), const(ident), const(expand), const(dup), const(fold)],
        out_specs=[tile(SSD_WIDTH),
                   pl.BlockSpec((None, hdr, SSD_CONV_CH), lambda b, t: (b, 0, 0)),
                   state_spec],
        out_shape=[jax.ShapeDtypeStruct((batch * seq, SSD_WIDTH), BF16),
                   jax.ShapeDtypeStruct((batch, hdr, SSD_CONV_CH), F32),
                   jax.ShapeDtypeStruct((batch, SSD_HEADS, SSD_HEAD_DIM, SSD_STATE), F32)],
        scratch_shapes=[pltpu.VMEM((tt + hdr, SSD_CONV_CH), F32), pltpu.VMEM((tt, SSD_CONV_CH), F32),
                        pltpu.VMEM((tt, V7X_LANES), F32), pltpu.VMEM((tt, V7X_LANES), F32),
                        pltpu.VMEM((npair, V7X_LANES, V7X_LANES), F32)],
        compiler_params=pltpu.CompilerParams(
            dimension_semantics=("arbitrary", "arbitrary"), vmem_limit_bytes=_vmem_limit(est)),
        name="ssd",
    )(xbc, z, dtga, conv_prev, s0, consts["conv_w"], consts["conv_b"], consts["dt_bias"], consts["a_log"],
      consts["d_exp"], consts["ssd_norm"], ltri, ident, expand, dup, fold)


def _gla_level_constants(cs):
    nl = int(np.log2(cs))
    idx = np.arange(cs)
    sums = np.zeros(((nl + 2) * cs, cs), np.float32)
    masks = np.zeros((nl + 1, cs, cs), np.float32)
    for lv in range(nl):
        s = 1 << lv
        blk = idx // (2 * s)
        right = (idx // s) % 2 == 1
        m = blk * 2 * s + s - 1
        for i in range(cs):
            if right[i]:
                sums[lv * cs + i, m[i] + 1:i + 1] = 1.0
            else:
                sums[lv * cs + i, i + 1:m[i] + 1] = 1.0
        masks[lv] = (blk[:, None] == blk[None, :]) & right[:, None] & ~right[None, :]
    masks[nl] = np.eye(cs)
    sums[nl * cs:(nl + 1) * cs] = np.tril(np.ones((cs, cs)))
    sums[(nl + 1) * cs:] = np.triu(np.ones((cs, cs)), 1)
    return nl, sums, masks


def _gla_kernel(gqk_ref, gv_ref, gr_ref, dtga_ref, s0_ref, wgate_ref, bgate_ref, norm_ref, sums_ref, masks_ref,
                ones_ref, y_ref, sout_ref, g_scr, st_scr, *, tt, cs, nl):
    t = pl.program_id(1)

    @pl.when(t == 0)
    def _():
        st_scr[...] = s0_ref[...]

    xg = jnp.dot(dtga_ref[...].astype(BF16), wgate_ref[...], preferred_element_type=F32) + bgate_ref[...]
    g_scr[...] = (jnp.minimum(xg, 0.0) - jnp.log1p(jnp.exp(-jnp.abs(xg)))) * (1.0 / GLA_TAU)

    qhead = lax.broadcasted_iota(jnp.int32, (cs, V7X_LANES), 1) // GLA_DK
    vhead = lax.broadcasted_iota(jnp.int32, (cs, GLA_WIDTH), 1) // GLA_DV
    bi = lax.broadcasted_iota(jnp.int32, (GLA_WIDTH, V7X_LANES), 0) // GLA_DV
    bj = lax.broadcasted_iota(jnp.int32, (GLA_WIDTH, V7X_LANES), 1) // GLA_DK
    blockdiag = bi == bj

    def chunk(c, carry):
        r0 = pl.multiple_of(c * cs, cs)
        g = g_scr[pl.ds(r0, cs), :]
        qk = gqk_ref[pl.ds(r0, cs), :]
        q = qk[:, :V7X_LANES] * (GLA_DK ** -0.5)
        k = qk[:, V7X_LANES:]
        v = gv_ref[pl.ds(r0, cs), :]
        g_hi = g.astype(BF16)
        g_lo = (g - g_hi.astype(F32)).astype(BF16)
        f2 = jnp.dot(sums_ref[...], jnp.concatenate([g_hi, g_lo], axis=1), preferred_element_type=F32)
        e = jnp.exp(f2[:, :V7X_LANES] + f2[:, V7X_LANES:])
        att = [None] * GLA_HEADS
        for lv in range(nl + 1):
            if lv < nl:
                e_lv = e[lv * cs:(lv + 1) * cs]
                qh, kh = q * e_lv, (k * e_lv).astype(BF16)
            else:
                qh, kh = q, k.astype(BF16)
            qstack = jnp.concatenate([jnp.where(qhead == h, qh, 0.0) for h in range(GLA_HEADS)], axis=0).astype(BF16)
            out = lax.dot_general(qstack, kh, _NT, preferred_element_type=F32)
            m = masks_ref[lv]
            for h in range(GLA_HEADS):
                part = m * out[h * cs:(h + 1) * cs]
                att[h] = part if att[h] is None else att[h] + part
        e_c = e[nl * cs:(nl + 1) * cs]
        e_rev = e[(nl + 1) * cs:(nl + 2) * cs]
        qt = (q * e_c).astype(BF16)
        kt = (k * e_rev).astype(BF16)
        a_cat = jnp.concatenate([a.astype(BF16) for a in att], axis=1)
        v_bd = jnp.concatenate([jnp.where(vhead == h, v, jnp.zeros_like(v)) for h in range(GLA_HEADS)], axis=0)
        st = st_scr[...]
        o = (jnp.dot(a_cat, v_bd, preferred_element_type=F32)
             + lax.dot_general(qt, st.astype(BF16), _NT, preferred_element_type=F32))
        upd = lax.dot_general(v, kt, _TN, preferred_element_type=F32)
        st_scr[...] = st * e_c[cs - 1:cs, :] + jnp.where(blockdiag, upd, 0.0)
        ms = _dot_exact_r(o * o, ones_ref[...]) * (1.0 / GLA_DV)
        r = gr_ref[pl.ds(r0, cs), :].astype(F32)
        y_ref[pl.ds(r0, cs), :] = (o * lax.rsqrt(ms + EPS) * norm_ref[...] * _silu(r)).astype(BF16)
        return carry

    lax.fori_loop(0, tt // cs, chunk, 0, unroll=True)

    @pl.when(t == pl.num_programs(1) - 1)
    def _():
        sout_ref[...] = st_scr[...]


def _gla_call(gqk, gv, gr, dtga, s0, consts, layer, batch, seq):
    tt = min(SEQ_TILE, seq)
    cs = min(SCAN_CHUNK, seq)
    nt = seq // tt
    nl, sums_np, masks_np = _gla_level_constants(cs)
    sums = jnp.asarray(sums_np, BF16)
    masks = jnp.asarray(masks_np, F32)
    ones = jnp.asarray(np.kron(np.eye(GLA_HEADS), np.ones((GLA_DV, GLA_DV))).astype(np.float32), BF16)
    tile = lambda w: pl.BlockSpec((tt, w), lambda b, t: (b * nt + t, 0))
    per_layer = lambda shape: _resident((None,) + shape, lambda b, t: (layer, 0, 0))
    const = lambda a: _resident(a.shape, lambda b, t: (0,) * a.ndim)
    est = 4 * (tt * (256 * 4 + 256 * 2 + 256 * 2 + 128 * 4 + 256 * 2)
               + 3 * (nl + 2) * cs * 128 * 4 + 12 * cs * cs * 4) + sums.size * 2 + masks.size * 4
    return pl.pallas_call(
        functools.partial(_gla_kernel, tt=tt, cs=cs, nl=nl),
        grid=(batch, nt),
        in_specs=[tile(2 * V7X_LANES), tile(GLA_WIDTH), tile(GLA_WIDTH), tile(V7X_LANES),
                  pl.BlockSpec((None, GLA_WIDTH, V7X_LANES), lambda b, t: (b, 0, 0)),
                  per_layer((V7X_LANES, V7X_LANES)), per_layer((1, V7X_LANES)), per_layer((1, GLA_WIDTH)),
                  const(sums), const(masks), const(ones)],
        out_specs=[tile(GLA_WIDTH), pl.BlockSpec((None, GLA_WIDTH, V7X_LANES), lambda b, t: (b, 0, 0))],
        out_shape=[jax.ShapeDtypeStruct((batch * seq, GLA_WIDTH), BF16),
                   jax.ShapeDtypeStruct((batch, GLA_WIDTH, V7X_LANES), F32)],
        scratch_shapes=[pltpu.VMEM((tt, V7X_LANES), F32), pltpu.VMEM((GLA_WIDTH, V7X_LANES), F32)],
        compiler_params=pltpu.CompilerParams(
            dimension_semantics=("arbitrary", "arbitrary"), vmem_limit_bytes=_vmem_limit(est)),
        name="gla",
    )(gqk, gv, gr, dtga, s0, consts["gla_w_gate"], consts["gla_b_gate"], consts["gla_norm"], sums, masks, ones)


N_MAPS = 2 * DIFF_HEADS


def _masked_queries(q):
    group = lax.broadcasted_iota(jnp.int32, q.shape, 1) // DIFF_DK
    return jnp.concatenate([jnp.where(group == hm, q, jnp.zeros_like(q)) for hm in range(N_MAPS)], axis=0)


def _values_with_ones(v):
    left = lax.broadcasted_iota(jnp.int32, (v.shape[0], V7X_LANES), 1) < DIFF_DV
    out = []
    for pr in range(DIFF_HEADS // 2):
        vp = v[:, V7X_LANES * pr:V7X_LANES * (pr + 1)]
        out.append(jnp.where(left, vp, 1.0).astype(BF16))
        out.append(jnp.where(left, 1.0, vp).astype(BF16))
    return out


def _values_with_ones_t(v_t):
    ones = jnp.ones((DIFF_DV, v_t.shape[1]), BF16)
    out = []
    for h in range(DIFF_HEADS):
        vh = v_t[h * DIFF_DV:(h + 1) * DIFF_DV, :].astype(BF16)
        out.append(jnp.concatenate([ones, vh] if h % 2 else [vh, ones], axis=0))
    return out


def _lane_tile(m, width):
    if width % V7X_LANES:
        return m[:, :width]
    return jnp.concatenate([m] * (width // V7X_LANES), axis=1)


def _attend(qm_all, k, vhs, m_prev, acc_prev, bias, transposed):
    tq = m_prev[0].shape[0]
    if transposed:
        s_all = jnp.dot(qm_all, k, preferred_element_type=F32)
    else:
        s_all = lax.dot_general(qm_all, k, _NT, preferred_element_type=F32)
    tk = s_all.shape[1]
    m_out, acc_out = [], []
    for h in range(DIFF_HEADS):
        ps, alphas = [], []
        for mp in range(2):
            hm = 2 * h + mp
            s = s_all[hm * tq:(hm + 1) * tq]
            if bias is not None:
                s = s + bias
            m_new = jnp.maximum(m_prev[hm], jnp.max(s, axis=-1, keepdims=True))
            m_out.append(m_new)
            alphas.append(jnp.exp2(m_prev[hm] - m_new))
            ps.append(jnp.exp2(s - _lane_tile(m_new, tk)).astype(BF16))
        p_both = jnp.concatenate(ps, axis=0)
        if transposed:
            pv = lax.dot_general(p_both, vhs[h], _NT, preferred_element_type=F32)
        else:
            pv = jnp.dot(p_both, vhs[h], preferred_element_type=F32)
        for mp in range(2):
            acc_out.append(alphas[mp] * acc_prev[2 * h + mp] + pv[mp * tq:(mp + 1) * tq])
    return m_out, acc_out


def _diff_finalize(accs, lam_ref, norm_ref, ones_ref, lam_init):
    lq = lam_ref[...]
    s1 = jnp.sum(lq[0:1, :] * lq[1:2, :], axis=-1, keepdims=True)
    s2 = jnp.sum(lq[2:3, :] * lq[3:4, :], axis=-1, keepdims=True)
    lam = jnp.exp(s1) - jnp.exp(s2) + lam_init
    tq = accs[0].shape[0]
    left = lax.broadcasted_iota(jnp.int32, (tq, V7X_LANES), 1) < DIFF_DV
    outs = []
    for pr in range(DIFF_HEADS // 2):
        ratio = []
        for mp in range(2):
            even, odd = accs[4 * pr + mp], accs[4 * pr + 2 + mp]
            num = jnp.where(left, even, odd)
            den = pltpu.roll(jnp.where(left, odd, even), DIFF_DV, axis=1)
            ratio.append(num / den)
        o = ratio[0] - lam * ratio[1]
        ms = _dot_exact_r(o * o, ones_ref[...]) * (1.0 / DIFF_DV)
        outs.append(o * lax.rsqrt(ms + EPS) * norm_ref[...] * (1.0 - lam_init))
    return jnp.concatenate(outs, axis=1)


_BOTH_FULL = 0
_FULL_DIAG = 1
_DIAG_ONLY = 2


def _attn_prompt_kernel(qi_ref, kb_ref, kind_ref, q_ref, k_ref, v_ref, lam_ref, norm_ref, bias_ref, ones_ref, o_ref,
                        qm_scr, m_scr, acc_scr, *, tile, lam_init):
    p = pl.program_id(1)
    kind = kind_ref[p]

    @pl.when(kb_ref[p] == 0)
    def _():
        qm_scr[...] = _masked_queries(q_ref[...])
        m_scr[...] = jnp.full(m_scr.shape, -jnp.inf, F32)
        acc_scr[...] = jnp.zeros(acc_scr.shape, F32)

    def run(diagonals, final):
        m = [m_scr[hm] for hm in range(N_MAPS)]
        acc = [acc_scr[hm] for hm in range(N_MAPS)]

        def attend(q_rows, m, acc, lo, hi, bias):
            return _attend(q_rows, k_ref[:, lo:hi].astype(BF16), _values_with_ones_t(v_ref[:, lo:hi]), m, acc, bias,
                           transposed=True)

        for half, diagonal in enumerate(diagonals):
            m, acc = attend(qm_scr[...], m, acc, half * tile, (half + 1) * tile, bias_ref[...] if diagonal else None)
        if final:
            o_ref[...] = _diff_finalize(acc, lam_ref, norm_ref, ones_ref, lam_init).astype(BF16)
        else:
            for hm in range(N_MAPS):
                m_scr[hm] = m[hm]
                acc_scr[hm] = acc[hm]

    @pl.when(kind == _BOTH_FULL)
    def _():
        run([False, False], final=False)

    @pl.when(kind == _FULL_DIAG)
    def _():
        run([False, True], final=True)

    @pl.when(kind == _DIAG_ONLY)
    def _():
        run([True], final=True)


def _chunk_causal_bias(tile):
    c = np.arange(tile) // CHUNK
    return jnp.asarray(np.where(c[None, :] <= c[:, None], 0.0, -np.inf), F32)


def _head_pair_ones():
    return jnp.asarray(np.kron(np.eye(V7X_LANES // DIFF_DV), np.ones((DIFF_DV, DIFF_DV))), BF16)


def _attn_prompt_call(q, k, v, lam_rows, norm, layer, batch, seq, lam_init):
    tile = min(ATTN_TILE, seq // 2)
    nq = seq // tile
    nkb = nq // 2
    assert seq == nkb * 2 * tile, (seq, tile)
    steps = []
    for a in range(nq):
        steps += [(a, kb, _BOTH_FULL) for kb in range(a // 2)]
        steps.append((a, a // 2, _FULL_DIAG if a % 2 else _DIAG_ONLY))
    qi, kb, kind = (jnp.asarray([s[j] for s in steps], jnp.int32) for j in range(3))
    qspec = pl.BlockSpec((tile, DIFF_WIDTH), lambda b, p, qi, kb, kind: (b * nq + qi[p], 0))
    kspec = pl.BlockSpec((None, None, DIFF_WIDTH, 2 * tile), lambda b, p, qi, kb, kind: (layer, b, 0, kb[p]))
    per_layer = lambda shape: pl.BlockSpec((None,) + shape, lambda b, p, qi, kb, kind: (layer, 0, 0))
    const = lambda a: pl.BlockSpec(a.shape, lambda b, p, qi, kb, kind: (0,) * a.ndim)
    bias, ones = _chunk_causal_bias(tile), _head_pair_ones()
    est = (4 * tile * DIFF_WIDTH * (2 + 4 + 4 + 2) + N_MAPS * tile * (DIFF_WIDTH * 2 + 2 * V7X_LANES * 4)
           + 5 * N_MAPS * tile * tile * 4 + 2 * tile * tile * 4)
    return pl.pallas_call(
        functools.partial(_attn_prompt_kernel, tile=tile, lam_init=lam_init),
        grid_spec=pltpu.PrefetchScalarGridSpec(
            num_scalar_prefetch=3,
            grid=(batch, len(steps)),
            in_specs=[qspec, kspec, kspec, per_layer((V7X_SUBLANES, V7X_LANES)), per_layer((1, V7X_LANES)),
                      const(bias), const(ones)],
            out_specs=qspec,
            scratch_shapes=[pltpu.VMEM((N_MAPS * tile, DIFF_WIDTH), BF16),
                            pltpu.VMEM((N_MAPS, tile, V7X_LANES), F32),
                            pltpu.VMEM((N_MAPS, tile, V7X_LANES), F32)]),
        out_shape=jax.ShapeDtypeStruct((batch * seq, DIFF_WIDTH), BF16),
        compiler_params=pltpu.CompilerParams(
            dimension_semantics=("arbitrary", "arbitrary"), vmem_limit_bytes=_vmem_limit(est)),
        name="attn_prompt",
    )(qi, kb, kind, q, k, v, lam_rows, norm, bias, ones)


def _attn_sample_kernel(q_ref, k_ref, v_ref, kp_ref, vp_ref, lam_ref, norm_ref, ones_ref, o_ref, *, lam_init):
    qm_all = _masked_queries(q_ref[...])
    tq = q_ref.shape[0]
    m = [jnp.full((tq, V7X_LANES), -jnp.inf, F32)] * N_MAPS
    acc = [jnp.zeros((tq, V7X_LANES), F32)] * N_MAPS
    m, acc = _attend(qm_all, kp_ref[...].astype(BF16), _values_with_ones_t(vp_ref[...]), m, acc, None,
                     transposed=True)
    m, acc = _attend(qm_all, k_ref[...].astype(BF16), _values_with_ones(v_ref[...]), m, acc, None,
                     transposed=False)
    o_ref[...] = _diff_finalize(acc, lam_ref, norm_ref, ones_ref, lam_init).astype(BF16)


def _attn_sample_call(q, k, v, k_past, v_past, lam_rows, norm, layer, batch, seq, lam_init):
    past = k_past.shape[3]
    qspec = pl.BlockSpec((seq, DIFF_WIDTH), lambda b: (b, 0))
    new = pl.BlockSpec((None, seq, DIFF_WIDTH), lambda b: (layer, b, 0))
    old = pl.BlockSpec((None, None, DIFF_WIDTH, past), lambda b: (layer, b, 0, 0))
    per_layer = lambda shape: pl.BlockSpec((None,) + shape, lambda b: (layer, 0, 0))
    ones = _head_pair_ones()
    est = 4 * past * DIFF_WIDTH * 4 + 3 * past * DIFF_WIDTH * 2 + 3 * N_MAPS * seq * past * 4
    return pl.pallas_call(
        functools.partial(_attn_sample_kernel, lam_init=lam_init),
        grid=(batch,),
        in_specs=[qspec, new, new, old, old, per_layer((V7X_SUBLANES, V7X_LANES)), per_layer((1, V7X_LANES)),
                  pl.BlockSpec(ones.shape, lambda b: (0, 0))],
        out_specs=qspec,
        out_shape=jax.ShapeDtypeStruct((batch * seq, DIFF_WIDTH), BF16),
        compiler_params=pltpu.CompilerParams(
            dimension_semantics=("arbitrary",), vmem_limit_bytes=_vmem_limit(est)),
        name="attn_sample",
    )(q, k, v, k_past, v_past, lam_rows, norm, ones)


def _pad_lanes(a, width):
    return jnp.pad(a, [(0, 0)] * (a.ndim - 1) + [(0, width - a.shape[-1])])


def _prepare_params(p):
    cols, start = [], 0
    for s in IN_SPLITS:
        cols.append(p["w_in"][..., start:start + s])
        start += s
    z, xbc, dt, gq, gk, gv, gr, ga, aq, ak, av = cols
    zeros = jnp.zeros(p["w_in"].shape[:2] + (_PAD_SMALL,), p["w_in"].dtype)
    w_in = jnp.concatenate([z, xbc, gq, gk, gv, gr, aq, dt, ga, zeros, ak, av], axis=-1).astype(BF16)
    w_kv_t = jnp.swapaxes(jnp.concatenate([ak, av], axis=-1), -1, -2).astype(BF16)
    row = lambda a: a[:, None, :]
    w_gate = jnp.zeros((DEPTH, V7X_LANES, V7X_LANES), F32)
    w_gate = w_gate.at[:, SSD_HEADS:SSD_HEADS + GLA_LOWRANK, :].set(p["gla_w_gate"]).astype(BF16)
    lam_rows = jnp.stack([p["diff_lambda_q1"], p["diff_lambda_k1"], p["diff_lambda_q2"], p["diff_lambda_k2"]], axis=1)
    lam_rows = jnp.pad(lam_rows, ((0, 0), (0, V7X_SUBLANES - 4), (0, V7X_LANES - DIFF_DK)))
    return dict(
        ffn1_norm=row(p["ffn1_norm"]), ffn2_norm=row(p["ffn2_norm"]), mix_norm=row(p["mix_norm"]),
        final_norm=p["final_norm"][None, :],
        ffn1=(p["ffn1_w_gate"].astype(BF16), p["ffn1_w_up"].astype(BF16), p["ffn1_w_down"].astype(BF16)),
        ffn2=(p["ffn2_w_gate"].astype(BF16), p["ffn2_w_up"].astype(BF16), p["ffn2_w_down"].astype(BF16)),
        w_in=w_in, w_kv_t=w_kv_t, w_out=p["w_out"].astype(BF16),
        conv_w=jnp.pad(p["ssd_conv_w"], ((0, 0), (0, V7X_SUBLANES - SSD_CONV), (0, 0))),
        conv_b=row(p["ssd_conv_b"]),
        dt_bias=row(_pad_lanes(p["ssd_dt_bias"], V7X_LANES)), a_log=row(_pad_lanes(p["ssd_a_log"], V7X_LANES)),
        d_exp=row(jnp.repeat(p["ssd_d"], SSD_HEAD_DIM, axis=-1)), ssd_norm=row(p["ssd_norm"]),
        gla_w_gate=w_gate, gla_b_gate=row(p["gla_b_gate"]), gla_norm=row(jnp.tile(p["gla_norm"], (1, GLA_HEADS))),
        lam_rows=lam_rows, diff_norm=row(jnp.tile(p["diff_norm"], (1, V7X_LANES // DIFF_DV))),
    )


def _gla_state_to_blocks(s):
    b = s.shape[0]
    st = jnp.swapaxes(s, -1, -2)
    eye = jnp.eye(GLA_HEADS, dtype=s.dtype)
    bd = st[:, :, :, None, :] * eye[None, :, None, :, None]
    return bd.reshape(b, GLA_HEADS * GLA_DV, GLA_HEADS * GLA_DK)


def _gla_state_from_blocks(bd):
    b = bd.shape[0]
    x = bd.reshape(b, GLA_HEADS, GLA_DV, GLA_HEADS, GLA_DK)
    st = jnp.stack([x[:, h, :, h, :] for h in range(GLA_HEADS)], axis=1)
    return jnp.swapaxes(st, -1, -2)


def _trunk(x, conv_st, ssd_st, gla_st, k_cache, v_cache, pp):
    batch, seq, d = x.shape
    prompt = k_cache is None
    x = x.reshape(batch * seq, d)
    convs, ssds, glas = [], [], []
    kv_bufs = None
    hdr = V7X_SUBLANES
    for l in range(DEPTH):
        lam_init = 0.8 - 0.6 * float(np.exp(-0.3 * l))
        x = _ffn_call(x, l, pp["ffn1_norm"], *pp["ffn1"])
        z, xbc, gqk, gv, gr, aq, dtga, ak, av = _inproj_call(
            x, l, pp["mix_norm"], pp["w_in"], pp["w_kv_t"], kv_bufs, batch, seq, kv_transposed=prompt)
        kv_bufs = (ak, av)
        if prompt:
            conv_prev = jnp.zeros((batch, hdr, SSD_CONV_CH), F32)
            ssd0 = jnp.zeros((batch, SSD_HEADS, SSD_HEAD_DIM, SSD_STATE), F32)
            gla0 = jnp.zeros((batch, GLA_WIDTH, V7X_LANES), F32)
        else:
            conv_prev = jnp.pad(conv_st[l], ((0, 0), (hdr - (SSD_CONV - 1), 0), (0, 0)))
            ssd0 = ssd_st[l].astype(F32)
            gla0 = _gla_state_to_blocks(gla_st[l].astype(F32))
        y_ssd, conv_out, ssd_out = _ssd_call(xbc, z, dtga, conv_prev, ssd0, pp, l, batch, seq)
        y_gla, gla_out = _gla_call(gqk, gv, gr, dtga, gla0, pp, l, batch, seq)
        if prompt:
            y_diff = _attn_prompt_call(aq, ak, av, pp["lam_rows"], pp["diff_norm"], l, batch, seq, lam_init)
        else:
            feature_major = lambda c: jnp.transpose(c, (0, 1, 3, 4, 2)).reshape(DEPTH, batch, DIFF_WIDTH, c.shape[2])
            y_diff = _attn_sample_call(aq, ak, av, feature_major(k_cache), feature_major(v_cache),
                                       pp["lam_rows"], pp["diff_norm"], l, batch, seq, lam_init)
        last = l == DEPTH - 1
        x = _ffn_call(x, l, pp["ffn2_norm"], *pp["ffn2"], mix=(y_ssd, y_gla, y_diff, pp["w_out"]),
                      final_norm=pp["final_norm"] if last else None)
        convs.append(conv_out[:, hdr - (SSD_CONV - 1):, :])
        ssds.append(ssd_out)
        glas.append(_gla_state_from_blocks(gla_out))
    if prompt:
        k_out, v_out = (jnp.transpose(a.reshape(DEPTH, batch, DIFF_HEADS, DIFF_DV, seq), (0, 1, 4, 2, 3))
                        for a in kv_bufs)
    else:
        k_out, v_out = (a.reshape(DEPTH, batch, seq, DIFF_HEADS, DIFF_DV) for a in kv_bufs)
    return (x.reshape(batch, seq, d), jnp.stack(convs), jnp.stack(ssds), jnp.stack(glas), k_out, v_out)


def kernel(x_prompt, x_sample, state_ssd_conv, state_ssd, state_gla, cache_diff_k, cache_diff_v, ffn1_norm, ffn1_w_gate, ffn1_w_up, ffn1_w_down, mix_norm, w_in, w_out, ssd_conv_w, ssd_conv_b, ssd_dt_bias, ssd_a_log, ssd_d, ssd_norm, gla_w_gate, gla_b_gate, gla_norm, diff_lambda_q1, diff_lambda_k1, diff_lambda_q2, diff_lambda_k2, diff_norm, ffn2_norm, ffn2_w_gate, ffn2_w_up, ffn2_w_down, final_norm):
    pp = _prepare_params(dict(
        ffn1_norm=ffn1_norm, ffn1_w_gate=ffn1_w_gate, ffn1_w_up=ffn1_w_up, ffn1_w_down=ffn1_w_down,
        mix_norm=mix_norm, w_in=w_in, w_out=w_out, ssd_conv_w=ssd_conv_w, ssd_conv_b=ssd_conv_b,
        ssd_dt_bias=ssd_dt_bias, ssd_a_log=ssd_a_log, ssd_d=ssd_d, ssd_norm=ssd_norm,
        gla_w_gate=gla_w_gate, gla_b_gate=gla_b_gate, gla_norm=gla_norm,
        diff_lambda_q1=diff_lambda_q1, diff_lambda_k1=diff_lambda_k1, diff_lambda_q2=diff_lambda_q2,
        diff_lambda_k2=diff_lambda_k2, diff_norm=diff_norm,
        ffn2_norm=ffn2_norm, ffn2_w_gate=ffn2_w_gate, ffn2_w_up=ffn2_w_up, ffn2_w_down=ffn2_w_down,
        final_norm=final_norm))
    y_p, conv_p, ssd_p, gla_p, k_p, v_p = _trunk(x_prompt, None, None, None, None, None, pp)
    y_s, conv_s, ssd_s, gla_s, k_s, v_s = _trunk(
        x_sample, state_ssd_conv, state_ssd, state_gla, cache_diff_k, cache_diff_v, pp)
    return (y_p, y_s, conv_p, conv_s, ssd_p, ssd_s, gla_p, gla_s, k_p, k_s, v_p, v_s)
```

```python
import functools

import jax
import jax.numpy as jnp
import numpy as np
from jax import lax
from jax.experimental import pallas as pl
from jax.experimental.pallas import tpu as pltpu

F32 = jnp.float32
BF16 = jnp.bfloat16

EPS = 1e-5
DEPTH = 4
D_MODEL = 1024
D_FF = 2816
CHUNK = 64

SSD_WIDTH = 512
SSD_HEADS = 8
SSD_HEAD_DIM = 64
SSD_GROUPS = 2
SSD_STATE = 64
SSD_CONV = 4
SSD_CONV_CH = SSD_WIDTH + 2 * SSD_GROUPS * SSD_STATE

GLA_WIDTH = 256
GLA_HEADS = 4
GLA_DK = 32
GLA_DV = 64
GLA_LOWRANK = 16
GLA_TAU = 16.0

DIFF_WIDTH = 256
DIFF_HEADS = 4
DIFF_DK = 32
DIFF_DV = 64

IN_SPLITS = (SSD_WIDTH, SSD_CONV_CH, SSD_HEADS,
             GLA_HEADS * GLA_DK, GLA_HEADS * GLA_DK, GLA_WIDTH, GLA_WIDTH, GLA_LOWRANK,
             DIFF_HEADS * 2 * DIFF_DK, DIFF_HEADS * 2 * DIFF_DK, DIFF_WIDTH)

V7X_LANES = 128
V7X_SUBLANES = 8
V7X_VMEM_BYTES = 64 * 1024 * 1024

_PAD_SMALL = V7X_LANES - SSD_HEADS - GLA_LOWRANK
IN_PADDED = sum(IN_SPLITS) + _PAD_SMALL
O_Z, O_XBC, O_GQ, O_GV, O_GR, O_AQ, O_DTGA, O_AK, O_AV = 0, 512, 1280, 1536, 1792, 2048, 2304, 2432, 2688

TM = 512
SEQ_TILE = 1024
SCAN_CHUNK = 128
ATTN_TILE = 512

_NT = (((1,), (1,)), ((), ()))
_TN = (((0,), (0,)), ((), ()))


def _vmem_limit(nbytes):
    return int(min(max(nbytes, 16 * 1024 * 1024), V7X_VMEM_BYTES - 6 * 1024 * 1024))


def _rmsnorm(x, g):
    ms = jnp.mean(x * x, axis=-1, keepdims=True)
    return x * lax.rsqrt(ms + EPS) * g


def _silu(x):
    half = 0.5 * x
    return half + half * jnp.tanh(half)


def _split3(x):
    hi = x.astype(BF16)
    r = x - hi.astype(F32)
    mid = r.astype(BF16)
    lo = (r - mid.astype(F32)).astype(BF16)
    return hi, mid, lo


def _dot_exact_l(a, x):
    return sum(jnp.dot(a, p, preferred_element_type=F32) for p in _split3(x))


def _dot_exact_r(x, a):
    return sum(jnp.dot(p, a, preferred_element_type=F32) for p in _split3(x))


def _dot_exact_nt(a, x):
    return sum(lax.dot_general(a, p, _NT, preferred_element_type=F32) for p in _split3(x))


def _resident(block_shape, index_map):
    return pl.BlockSpec(block_shape, index_map, pipeline_mode=pl.Buffered(1))


V7X_MXU_DIM = 256
_FF_CUT = (D_FF // V7X_MXU_DIM + 1) // 2 * V7X_MXU_DIM
FF_BLOCKS = ((0, _FF_CUT), (_FF_CUT, D_FF))


def _ffn_kernel(*refs, with_mix, with_final):
    it = iter(refs)
    x_ref = next(it)
    if with_mix:
        y1_ref, y2_ref, y3_ref, wo_ref = next(it), next(it), next(it), next(it)
    g_ref, wg_ref, wu_ref, wd_ref = next(it), next(it), next(it), next(it)
    if with_final:
        fn_ref = next(it)
    o_ref = next(it)

    x = x_ref[...]
    if with_mix:
        y = jnp.concatenate([y1_ref[...], y2_ref[...], y3_ref[...]], axis=-1)
        x = x + jnp.dot(y, wo_ref[...], preferred_element_type=F32)
    h = _rmsnorm(x, g_ref[...]).astype(BF16)
    acc = None
    for lo, hi in FF_BLOCKS:
        gate = jnp.dot(h, wg_ref[:, lo:hi], preferred_element_type=F32)
        up = jnp.dot(h, wu_ref[:, lo:hi], preferred_element_type=F32)
        act = (_silu(gate) * up).astype(BF16)
        part = jnp.dot(act, wd_ref[lo:hi, :], preferred_element_type=F32)
        acc = part if acc is None else acc + part
    x = x + 0.5 * acc
    if with_final:
        x = _rmsnorm(x, fn_ref[...])
    o_ref[...] = x


def _ffn_call(x, layer, norm, wg, wu, wd, mix=None, final_norm=None):
    t, d = x.shape
    tm = min(TM, t)
    tile = lambda w: pl.BlockSpec((tm, w), lambda i: (i, 0))
    per_layer = lambda shape: _resident((None,) + shape, lambda i: (layer, 0, 0))
    args, specs = [x], [tile(d)]
    if mix is not None:
        y1, y2, y3, wo = mix
        args += [y1, y2, y3, wo]
        specs += [tile(y1.shape[1]), tile(y2.shape[1]), tile(y3.shape[1]), per_layer((d, d))]
    args += [norm, wg, wu, wd]
    specs += [per_layer((1, d)), per_layer((d, D_FF)), per_layer((d, D_FF)), per_layer((D_FF, d))]
    if final_norm is not None:
        args.append(final_norm)
        specs.append(_resident((1, d), lambda i: (0, 0)))
    weights = 2 * (3 * d * D_FF + d * d)
    tiles = 2 * (2 * tm * d * 4 + tm * d * 2)
    temps = tm * (3 * d * 4 + _FF_CUT * (4 + 4 + 2))
    return pl.pallas_call(
        functools.partial(_ffn_kernel, with_mix=mix is not None, with_final=final_norm is not None),
        grid=(t // tm,),
        in_specs=specs,
        out_specs=tile(d),
        out_shape=jax.ShapeDtypeStruct((t, d), F32),
        compiler_params=pltpu.CompilerParams(
            dimension_semantics=("arbitrary",), vmem_limit_bytes=_vmem_limit(weights + tiles + 2 * temps)),
        name="ffn",
    )(*args)


def _inproj_kernel(x_ref, g_ref, w_ref, wt_ref, k_in_ref, v_in_ref,
                   z_ref, xbc_ref, gqk_ref, gv_ref, gr_ref, aq_ref, dtga_ref, ak_ref, av_ref, *, kv_transposed):
    del k_in_ref, v_in_ref
    h = _rmsnorm(x_ref[...], g_ref[...]).astype(BF16)
    if kv_transposed:
        u = jnp.dot(h, w_ref[:, :O_AK], preferred_element_type=F32)
        kv_t = lax.dot_general(wt_ref[...], h, _NT, preferred_element_type=F32)
        ak_ref[...] = kv_t[:DIFF_WIDTH]
        av_ref[...] = kv_t[DIFF_WIDTH:]
    else:
        u = jnp.dot(h, w_ref[...], preferred_element_type=F32)
        ak_ref[...] = u[:, O_AK:O_AV]
        av_ref[...] = u[:, O_AV:IN_PADDED]
    z_ref[...] = u[:, O_Z:O_XBC].astype(BF16)
    xbc_ref[...] = u[:, O_XBC:O_GQ]
    gqk_ref[...] = u[:, O_GQ:O_GV]
    gv_ref[...] = u[:, O_GV:O_GR].astype(BF16)
    gr_ref[...] = u[:, O_GR:O_AQ].astype(BF16)
    aq_ref[...] = (u[:, O_AQ:O_DTGA] * (DIFF_DK ** -0.5 * np.log2(np.e))).astype(BF16)
    dtga_ref[...] = u[:, O_DTGA:O_AK]


def _inproj_call(x, layer, norm, w, w_kv_t, kv_bufs, batch, seq, kv_transposed):
    t, d = x.shape
    widths = [(O_XBC - O_Z, BF16), (O_GQ - O_XBC, F32), (O_GV - O_GQ, F32), (O_GR - O_GV, BF16),
              (O_AQ - O_GR, BF16), (O_DTGA - O_AQ, BF16), (O_AK - O_DTGA, F32)]
    tm = min(TM, t)
    tile = lambda w_: pl.BlockSpec((tm, w_), lambda i: (i, 0))
    per_layer = lambda shape: _resident((None,) + shape, lambda i: (layer, 0, 0))
    out_specs = [tile(w_) for w_, _ in widths]
    out_shape = [jax.ShapeDtypeStruct((t, w_), dt) for w_, dt in widths]
    if kv_transposed:
        nt = seq // tm
        out_specs += [pl.BlockSpec((None, None, DIFF_WIDTH, tm), lambda i: (layer, i // nt, 0, i % nt))] * 2
        out_shape += [jax.ShapeDtypeStruct((DEPTH, batch, DIFF_WIDTH, seq), F32)] * 2
    else:
        out_specs += [pl.BlockSpec((None, tm, DIFF_WIDTH), lambda i: (layer, i, 0))] * 2
        out_shape += [jax.ShapeDtypeStruct((DEPTH, t, DIFF_WIDTH), F32)] * 2
    args = [x, norm, w, w_kv_t, *kv_bufs]
    in_specs = [tile(d), per_layer((1, d)), per_layer((d, IN_PADDED)), per_layer((2 * DIFF_WIDTH, d)),
                pl.BlockSpec(memory_space=pl.ANY), pl.BlockSpec(memory_space=pl.ANY)]
    aliases = {len(args) - 2: len(widths), len(args) - 1: len(widths) + 1}
    est = 2 * d * (IN_PADDED + 2 * DIFF_WIDTH) + 2 * tm * d * 4 + 3 * tm * IN_PADDED * 4
    return pl.pallas_call(
        functools.partial(_inproj_kernel, kv_transposed=kv_transposed),
        grid=(t // tm,),
        in_specs=in_specs,
        out_specs=out_specs,
        out_shape=out_shape,
        input_output_aliases=aliases,
        compiler_params=pltpu.CompilerParams(
            dimension_semantics=("arbitrary",), vmem_limit_bytes=_vmem_limit(est)),
        name="inproj",
    )(*args)


def _ssd_kernel(xbc_ref, z_ref, dtga_ref, cprev_ref, s0_ref, cw_ref, cb_ref, dtb_ref, alog_ref, dexp_ref,
                norm_ref, ltri_ref, ident_ref, expand_ref, dup_ref, fold_ref,
                y_ref, cout_ref, sout_ref,
                xp_scr, conv_scr, a_scr, dt_scr, s_scr, *, tt, cs):
    t = pl.program_id(1)
    hdr = V7X_SUBLANES
    bi = lax.broadcasted_iota(jnp.int32, (V7X_LANES, V7X_LANES), 0) // SSD_HEAD_DIM
    bj = lax.broadcasted_iota(jnp.int32, (V7X_LANES, V7X_LANES), 1) // SSD_STATE
    blockdiag = bi == bj

    @pl.when(t == 0)
    def _():
        xp_scr[0:hdr, :] = cprev_ref[...]
        for pr in range(SSD_HEADS // 2):
            both = jnp.concatenate([s0_ref[2 * pr], s0_ref[2 * pr + 1]], axis=0)
            s_scr[pr] = jnp.where(blockdiag, _dot_exact_r(both, dup_ref[...]), 0.0)

    xp_scr[hdr:hdr + tt, :] = xbc_ref[...]
    x3 = xp_scr[...].reshape((tt + hdr) // hdr, hdr, SSD_CONV_CH)
    sub = lax.broadcasted_iota(jnp.int32, (1, hdr, SSD_CONV_CH), 1)
    tap = lambda i: cw_ref[i:i + 1, :].reshape(1, 1, SSD_CONV_CH)
    conv = cb_ref[...].reshape(1, 1, SSD_CONV_CH) + x3[1:] * tap(SSD_CONV - 1)
    for d in range(1, SSD_CONV):
        rot = pltpu.roll(x3, d, axis=1)
        conv = conv + jnp.where(sub < d, rot[:-1], rot[1:]) * tap(SSD_CONV - 1 - d)
    conv_scr[...] = _silu(conv).reshape(tt, SSD_CONV_CH)
    xp_scr[0:hdr, :] = xp_scr[tt:tt + hdr, :]

    lane = lax.broadcasted_iota(jnp.int32, (tt, V7X_LANES), 1)
    x_dt = dtga_ref[...] + dtb_ref[...]
    dt = jnp.maximum(x_dt, 0.0) + jnp.log1p(jnp.exp(-jnp.abs(x_dt)))
    dt = jnp.where(lane < SSD_HEADS, dt, 0.0)
    dt_scr[...] = dt
    a_scr[...] = dt * (-jnp.exp(alog_ref[...]))

    left = lax.broadcasted_iota(jnp.int32, (cs, V7X_LANES), 1) < SSD_STATE
    ri = lax.broadcasted_iota(jnp.int32, (cs, cs), 0)
    ci = lax.broadcasted_iota(jnp.int32, (cs, cs), 1)
    causal = ci <= ri
    def chunk(c, carry):
        r0 = pl.multiple_of(c * cs, cs)
        conv_c = conv_scr[pl.ds(r0, cs), :]
        xs = conv_c[:, :SSD_WIDTH]
        bm = conv_c[:, SSD_WIDTH:SSD_WIDTH + V7X_LANES]
        cm = conv_c[:, SSD_WIDTH + V7X_LANES:]
        a_c = a_scr[pl.ds(r0, cs), :]
        dt_c = dt_scr[pl.ds(r0, cs), :]
        cum = _dot_exact_l(ltri_ref[...], a_c)
        cum_t = _dot_exact_nt(ident_ref[...], cum)
        last = cum[cs - 1:cs, :]
        e_cum = jnp.exp(cum)
        e_rev = jnp.exp(last - cum)
        small = jnp.concatenate([dt_c, e_cum, e_rev], axis=0).astype(BF16)
        wide = jnp.dot(small, expand_ref[...], preferred_element_type=F32)
        dt_x, ecum_x, erev_x = wide[0:cs], wide[cs:2 * cs], wide[2 * cs:3 * cs]
        elast_x = _dot_exact_r(jnp.broadcast_to(e_cum[cs - 1:cs, :], (V7X_SUBLANES, V7X_LANES)),
                               expand_ref[...])[0:1, :]
        xd = xs * dt_x
        bm_sw = pltpu.roll(bm, SSD_STATE, axis=1)
        cm_sw = pltpu.roll(cm, SSD_STATE, axis=1)
        ys = []
        for g in range(SSD_GROUPS):
            if g == 0:
                bdup, cdup, cmask = jnp.where(left, bm, bm_sw), jnp.where(left, cm, cm_sw), jnp.where(left, cm, 0.0)
            else:
                bdup, cdup, cmask = jnp.where(left, bm_sw, bm), jnp.where(left, cm_sw, cm), jnp.where(left, 0.0, cm)
            gmat = lax.dot_general(cmask.astype(BF16), bm.astype(BF16), _NT, preferred_element_type=F32)
            for pr in (2 * g, 2 * g + 1):
                sl = slice(V7X_LANES * pr, V7X_LANES * (pr + 1))
                xd_p = xd[:, sl]
                top = jnp.where(left, xd_p, 0.0).astype(BF16)
                bot = jnp.where(left, 0.0, xd_p).astype(BF16)
                scores = []
                for hd in range(2):
                    h = 2 * pr + hd
                    arg = cum[:, h:h + 1] - cum_t[h:h + 1, :]
                    decay = jnp.where(causal, jnp.exp(jnp.minimum(arg, 0.0)), 0.0)
                    scores.append((gmat * decay).astype(BF16))
                ce = (cdup * ecum_x[:, sl]).astype(BF16)
                s_pair = s_scr[pr]
                ys.append(jnp.dot(jnp.concatenate(scores, axis=1), jnp.concatenate([top, bot], axis=0),
                                  preferred_element_type=F32)
                          + lax.dot_general(ce, s_pair.astype(BF16), _NT, preferred_element_type=F32))
                bt = (bdup * erev_x[:, sl]).astype(BF16)
                upd = lax.dot_general(xd_p.astype(BF16), bt, _TN, preferred_element_type=F32)
                s_scr[pr] = s_pair * elast_x[:, sl] + jnp.where(blockdiag, upd, 0.0)
        y = jnp.concatenate(ys, axis=1) + dexp_ref[...] * xs
        y = y * _silu(z_ref[pl.ds(r0, cs), :].astype(F32))
        y_ref[pl.ds(r0, cs), :] = _rmsnorm(y, norm_ref[...]).astype(BF16)
        return carry

    lax.fori_loop(0, tt // cs, chunk, 0, unroll=True)

    @pl.when(t == pl.num_programs(1) - 1)
    def _():
        cout_ref[...] = xp_scr[0:hdr, :]
        for pr in range(SSD_HEADS // 2):
            both = _dot_exact_r(s_scr[pr], fold_ref[...])
            sout_ref[2 * pr] = both[:SSD_HEAD_DIM]
            sout_ref[2 * pr + 1] = both[SSD_HEAD_DIM:]


def _ssd_call(xbc, z, dtga, conv_prev, s0, consts, layer, batch, seq):
    tt = min(SEQ_TILE, seq)
    cs = min(SCAN_CHUNK, seq)
    nt = seq // tt
    hdr = V7X_SUBLANES
    tile = lambda w: pl.BlockSpec((tt, w), lambda b, t: (b * nt + t, 0))
    per_layer = lambda shape: _resident((None,) + shape, lambda b, t: (layer, 0, 0))
    const = lambda a: _resident(a.shape, lambda b, t: (0,) * a.ndim)
    npair = SSD_HEADS // 2
    ltri = jnp.asarray(np.tril(np.ones((cs, cs), np.float32)), BF16)
    ident = jnp.asarray(np.eye(V7X_LANES, dtype=np.float32), BF16)
    ex = np.zeros((V7X_LANES, SSD_WIDTH), np.float32)
    for h in range(SSD_HEADS):
        ex[h, h * SSD_HEAD_DIM:(h + 1) * SSD_HEAD_DIM] = 1.0
    expand = jnp.asarray(ex, BF16)
    eye = np.eye(SSD_STATE, dtype=np.float32)
    dup = jnp.asarray(np.concatenate([eye, eye], axis=1), BF16)
    fold = jnp.asarray(np.concatenate([eye, eye], axis=0), BF16)
    state_spec = pl.BlockSpec((None, SSD_HEADS, SSD_HEAD_DIM, SSD_STATE), lambda b, t: (b, 0, 0, 0))
    est = 4 * (tt * (SSD_CONV_CH * 4 + SSD_WIDTH * 2 + 128 * 4 + SSD_WIDTH * 2)
               + (tt + hdr) * SSD_CONV_CH * 4 + tt * SSD_CONV_CH * 4 + 16 * cs * cs * 4)
    return pl.pallas_call(
        functools.partial(_ssd_kernel, tt=tt, cs=cs),
        grid=(batch, nt),
        in_specs=[tile(SSD_CONV_CH), tile(SSD_WIDTH), tile(V7X_LANES),
                  pl.BlockSpec((None, hdr, SSD_CONV_CH), lambda b, t: (b, 0, 0)),
                  state_spec,
                  per_layer((hdr, SSD_CONV_CH)), per_layer((1, SSD_CONV_CH)), per_layer((1, V7X_LANES)),
                  per_layer((1, V7X_LANES)), per_layer((1, SSD_WIDTH)), per_layer((1, SSD_WIDTH)),
                  const(ltri), const(ident), const(expand), const(dup), const(fold)],
        out_specs=[tile(SSD_WIDTH),
                   pl.BlockSpec((None, hdr, SSD_CONV_CH), lambda b, t: (b, 0, 0)),
                   state_spec],
        out_shape=[jax.ShapeDtypeStruct((batch * seq, SSD_WIDTH), BF16),
                   jax.ShapeDtypeStruct((batch, hdr, SSD_CONV_CH), F32),
                   jax.ShapeDtypeStruct((batch, SSD_HEADS, SSD_HEAD_DIM, SSD_STATE), F32)],
        scratch_shapes=[pltpu.VMEM((tt + hdr, SSD_CONV_CH), F32), pltpu.VMEM((tt, SSD_CONV_CH), F32),
                        pltpu.VMEM((tt, V7X_LANES), F32), pltpu.VMEM((tt, V7X_LANES), F32),
                        pltpu.VMEM((npair, V7X_LANES, V7X_LANES), F32)],
        compiler_params=pltpu.CompilerParams(
            dimension_semantics=("arbitrary", "arbitrary"), vmem_limit_bytes=_vmem_limit(est)),
        name="ssd",
    )(xbc, z, dtga, conv_prev, s0, consts["conv_w"], consts["conv_b"], consts["dt_bias"], consts["a_log"],
      consts["d_exp"], consts["ssd_norm"], ltri, ident, expand, dup, fold)


def _gla_level_constants(cs):
    nl = int(np.log2(cs))
    idx = np.arange(cs)
    sums = np.zeros(((nl + 2) * cs, cs), np.float32)
    masks = np.zeros((nl + 1, cs, cs), np.float32)
    for lv in range(nl):
        s = 1 << lv
        blk = idx // (2 * s)
        right = (idx // s) % 2 == 1
        m = blk * 2 * s + s - 1
        for i in range(cs):
            if right[i]:
                sums[lv * cs + i, m[i] + 1:i + 1] = 1.0
            else:
                sums[lv * cs + i, i + 1:m[i] + 1] = 1.0
        masks[lv] = (blk[:, None] == blk[None, :]) & right[:, None] & ~right[None, :]
    masks[nl] = np.eye(cs)
    sums[nl * cs:(nl + 1) * cs] = np.tril(np.ones((cs, cs)))
    sums[(nl + 1) * cs:] = np.triu(np.ones((cs, cs)), 1)
    return nl, sums, masks


def _gla_kernel(gqk_ref, gv_ref, gr_ref, dtga_ref, s0_ref, wgate_ref, bgate_ref, norm_ref, sums_ref, masks_ref,
                ones_ref, y_ref, sout_ref, g_scr, st_scr, *, tt, cs, nl):
    t = pl.program_id(1)

    @pl.when(t == 0)
    def _():
        st_scr[...] = s0_ref[...]

    xg = jnp.dot(dtga_ref[...].astype(BF16), wgate_ref[...], preferred_element_type=F32) + bgate_ref[...]
    g_scr[...] = (jnp.minimum(xg, 0.0) - jnp.log1p(jnp.exp(-jnp.abs(xg)))) * (1.0 / GLA_TAU)

    qhead = lax.broadcasted_iota(jnp.int32, (cs, V7X_LANES), 1) // GLA_DK
    vhead = lax.broadcasted_iota(jnp.int32, (cs, GLA_WIDTH), 1) // GLA_DV
    bi = lax.broadcasted_iota(jnp.int32, (GLA_WIDTH, V7X_LANES), 0) // GLA_DV
    bj = lax.broadcasted_iota(jnp.int32, (GLA_WIDTH, V7X_LANES), 1) // GLA_DK
    blockdiag = bi == bj

    def chunk(c, carry):
        r0 = pl.multiple_of(c * cs, cs)
        g = g_scr[pl.ds(r0, cs), :]
        qk = gqk_ref[pl.ds(r0, cs), :]
        q = qk[:, :V7X_LANES] * (GLA_DK ** -0.5)
        k = qk[:, V7X_LANES:]
        v = gv_ref[pl.ds(r0, cs), :]
        g_hi = g.astype(BF16)
        g_lo = (g - g_hi.astype(F32)).astype(BF16)
        f2 = jnp.dot(sums_ref[...], jnp.concatenate([g_hi, g_lo], axis=1), preferred_element_type=F32)
        e = jnp.exp(f2[:, :V7X_LANES] + f2[:, V7X_LANES:])
        att = [None] * GLA_HEADS
        for lv in range(nl + 1):
            if lv < nl:
                e_lv = e[lv * cs:(lv + 1) * cs]
                qh, kh = q * e_lv, (k * e_lv).astype(BF16)
            else:
                qh, kh = q, k.astype(BF16)
            qstack = jnp.concatenate([jnp.where(qhead == h, qh, 0.0) for h in range(GLA_HEADS)], axis=0).astype(BF16)
            out = lax.dot_general(qstack, kh, _NT, preferred_element_type=F32)
            m = masks_ref[lv]
            for h in range(GLA_HEADS):
                part = m * out[h * cs:(h + 1) * cs]
                att[h] = part if att[h] is None else att[h] + part
        e_c = e[nl * cs:(nl + 1) * cs]
        e_rev = e[(nl + 1) * cs:(nl + 2) * cs]
        qt = (q * e_c).astype(BF16)
        kt = (k * e_rev).astype(BF16)
        a_cat = jnp.concatenate([a.astype(BF16) for a in att], axis=1)
        v_bd = jnp.concatenate([jnp.where(vhead == h, v, jnp.zeros_like(v)) for h in range(GLA_HEADS)], axis=0)
        st = st_scr[...]
        o = (jnp.dot(a_cat, v_bd, preferred_element_type=F32)
             + lax.dot_general(qt, st.astype(BF16), _NT, preferred_element_type=F32))
        upd = lax.dot_general(v, kt, _TN, preferred_element_type=F32)
        st_scr[...] = st * e_c[cs - 1:cs, :] + jnp.where(blockdiag, upd, 0.0)
        ms = _dot_exact_r(o * o, ones_ref[...]) * (1.0 / GLA_DV)
        r = gr_ref[pl.ds(r0, cs), :].astype(F32)
        y_ref[pl.ds(r0, cs), :] = (o * lax.rsqrt(ms + EPS) * norm_ref[...] * _silu(r)).astype(BF16)
        return carry

    lax.fori_loop(0, tt // cs, chunk, 0, unroll=True)

    @pl.when(t == pl.num_programs(1) - 1)
    def _():
        sout_ref[...] = st_scr[...]


def _gla_call(gqk, gv, gr, dtga, s0, consts, layer, batch, seq):
    tt = min(SEQ_TILE, seq)
    cs = min(SCAN_CHUNK, seq)
    nt = seq // tt
    nl, sums_np, masks_np = _gla_level_constants(cs)
    sums = jnp.asarray(sums_np, BF16)
    masks = jnp.asarray(masks_np, F32)
    ones = jnp.asarray(np.kron(np.eye(GLA_HEADS), np.ones((GLA_DV, GLA_DV))).astype(np.float32), BF16)
    tile = lambda w: pl.BlockSpec((tt, w), lambda b, t: (b * nt + t, 0))
    per_layer = lambda shape: _resident((None,) + shape, lambda b, t: (layer, 0, 0))
    const = lambda a: _resident(a.shape, lambda b, t: (0,) * a.ndim)
    est = 4 * (tt * (256 * 4 + 256 * 2 + 256 * 2 + 128 * 4 + 256 * 2)
               + 3 * (nl + 2) * cs * 128 * 4 + 12 * cs * cs * 4) + sums.size * 2 + masks.size * 4
    return pl.pallas_call(
        functools.partial(_gla_kernel, tt=tt, cs=cs, nl=nl),
        grid=(batch, nt),
        in_specs=[tile(2 * V7X_LANES), tile(GLA_WIDTH), tile(GLA_WIDTH), tile(V7X_LANES),
                  pl.BlockSpec((None, GLA_WIDTH, V7X_LANES), lambda b, t: (b, 0, 0)),
                  per_layer((V7X_LANES, V7X_LANES)), per_layer((1, V7X_LANES)), per_layer((1, GLA_WIDTH)),
                  const(sums), const(masks), const(ones)],
        out_specs=[tile(GLA_WIDTH), pl.BlockSpec((None, GLA_WIDTH, V7X_LANES), lambda b, t: (b, 0, 0))],
        out_shape=[jax.ShapeDtypeStruct((batch * seq, GLA_WIDTH), BF16),
                   jax.ShapeDtypeStruct((batch, GLA_WIDTH, V7X_LANES), F32)],
        scratch_shapes=[pltpu.VMEM((tt, V7X_LANES), F32), pltpu.VMEM((GLA_WIDTH, V7X_LANES), F32)],
        compiler_params=pltpu.CompilerParams(
            dimension_semantics=("arbitrary", "arbitrary"), vmem_limit_bytes=_vmem_limit(est)),
        name="gla",
    )(gqk, gv, gr, dtga, s0, consts["gla_w_gate"], consts["gla_b_gate"], consts["gla_norm"], sums, masks, ones)


N_MAPS = 2 * DIFF_HEADS


def _masked_queries(q):
    group = lax.broadcasted_iota(jnp.int32, q.shape, 1) // DIFF_DK
    return jnp.concatenate([jnp.where(group == hm, q, jnp.zeros_like(q)) for hm in range(N_MAPS)], axis=0)


def _values_with_ones(v):
    left = lax.broadcasted_iota(jnp.int32, (v.shape[0], V7X_LANES), 1) < DIFF_DV
    out = []
    for pr in range(DIFF_HEADS // 2):
        vp = v[:, V7X_LANES * pr:V7X_LANES * (pr + 1)]
        out.append(jnp.where(left, vp, 1.0).astype(BF16))
        out.append(jnp.where(left, 1.0, vp).astype(BF16))
    return out


def _values_with_ones_t(v_t):
    ones = jnp.ones((DIFF_DV, v_t.shape[1]), BF16)
    out = []
    for h in range(DIFF_HEADS):
        vh = v_t[h * DIFF_DV:(h + 1) * DIFF_DV, :].astype(BF16)
        out.append(jnp.concatenate([ones, vh] if h % 2 else [vh, ones], axis=0))
    return out


def _lane_tile(m, width):
    if width % V7X_LANES:
        return m[:, :width]
    return jnp.concatenate([m] * (width // V7X_LANES), axis=1)


def _attend(qm_all, k, vhs, m_prev, acc_prev, bias, transposed):
    tq = qm_all.shape[0] // N_MAPS
    if transposed:
        s_all = jnp.dot(qm_all, k, preferred_element_type=F32)
    else:
        s_all = lax.dot_general(qm_all, k, _NT, preferred_element_type=F32)
    tk = s_all.shape[1]
    m_out, acc_out = [], []
    for h in range(DIFF_HEADS):
        ps, alphas = [], []
        for mp in range(2):
            hm = 2 * h + mp
            s = s_all[hm * tq:(hm + 1) * tq]
            if bias is not None:
                s = s + bias
            m_new = jnp.max(s, axis=-1, keepdims=True)
            if m_prev is None:
                m_new = jnp.broadcast_to(m_new, (tq, V7X_LANES))
            else:
                m_new = jnp.maximum(m_prev[hm], m_new)
                alphas.append(jnp.exp2(m_prev[hm] - m_new))
            m_out.append(m_new)
            ps.append(jnp.exp2(s - _lane_tile(m_new, tk)).astype(BF16))
        p_both = jnp.concatenate(ps, axis=0)
        if transposed:
            pv = lax.dot_general(p_both, vhs[h], _NT, preferred_element_type=F32)
        else:
            pv = jnp.dot(p_both, vhs[h], preferred_element_type=F32)
        for mp in range(2):
            new = pv[mp * tq:(mp + 1) * tq]
            acc_out.append(new if m_prev is None else alphas[mp] * acc_prev[2 * h + mp] + new)
    return m_out, acc_out


def _diff_finalize(accs, lam_ref, norm_ref, ones_ref, lam_init):
    lq = lam_ref[...]
    s1 = jnp.sum(lq[0:1, :] * lq[1:2, :], axis=-1, keepdims=True)
    s2 = jnp.sum(lq[2:3, :] * lq[3:4, :], axis=-1, keepdims=True)
    lam = jnp.exp(s1) - jnp.exp(s2) + lam_init
    tq = accs[0].shape[0]
    left = lax.broadcasted_iota(jnp.int32, (tq, V7X_LANES), 1) < DIFF_DV
    outs = []
    for pr in range(DIFF_HEADS // 2):
        ratio = []
        for mp in range(2):
            even, odd = accs[4 * pr + mp], accs[4 * pr + 2 + mp]
            num = jnp.where(left, even, odd)
            den = pltpu.roll(jnp.where(left, odd, even), DIFF_DV, axis=1)
            ratio.append(num / den)
        o = ratio[0] - lam * ratio[1]
        ms = _dot_exact_r(o * o, ones_ref[...]) * (1.0 / DIFF_DV)
        outs.append(o * lax.rsqrt(ms + EPS) * norm_ref[...] * (1.0 - lam_init))
    return jnp.concatenate(outs, axis=1)


_BOTH_FULL = 0
_FULL_DIAG = 1
_DIAG_ONLY = 2
_BOTH_FULL_FIRST = 3


def _attn_prompt_kernel(qi_ref, kb_ref, kind_ref, q_ref, k_ref, v_ref, lam_ref, norm_ref, bias_ref, ones_ref, o_ref,
                        qm_scr, m_scr, acc_scr, *, tile, lam_init):
    p = pl.program_id(1)
    kind = kind_ref[p]

    @pl.when(kb_ref[p] == 0)
    def _():
        qm_scr[...] = _masked_queries(q_ref[...])

    @pl.when((kb_ref[p] == 0) & (kind != _BOTH_FULL_FIRST))
    def _():
        m_scr[...] = jnp.full(m_scr.shape, -jnp.inf, F32)
        acc_scr[...] = jnp.zeros(acc_scr.shape, F32)

    def run(diagonals, final, first=False):
        m = None if first else [m_scr[hm] for hm in range(N_MAPS)]
        acc = None if first else [acc_scr[hm] for hm in range(N_MAPS)]

        def attend(q_rows, m, acc, lo, hi, bias):
            return _attend(q_rows, k_ref[:, lo:hi].astype(BF16), _values_with_ones_t(v_ref[:, lo:hi]), m, acc, bias,
                           transposed=True)

        for half, diagonal in enumerate(diagonals):
            m, acc = attend(qm_scr[...], m, acc, half * tile, (half + 1) * tile, bias_ref[...] if diagonal else None)
        if final:
            o_ref[...] = _diff_finalize(acc, lam_ref, norm_ref, ones_ref, lam_init).astype(BF16)
        else:
            for hm in range(N_MAPS):
                m_scr[hm] = m[hm]
                acc_scr[hm] = acc[hm]

    @pl.when(kind == _BOTH_FULL)
    def _():
        run([False, False], final=False)

    @pl.when(kind == _BOTH_FULL_FIRST)
    def _():
        run([False, False], final=False, first=True)

    @pl.when(kind == _FULL_DIAG)
    def _():
        run([False, True], final=True)

    @pl.when(kind == _DIAG_ONLY)
    def _():
        run([True], final=True)


def _chunk_causal_bias(tile):
    c = np.arange(tile) // CHUNK
    return jnp.asarray(np.where(c[None, :] <= c[:, None], 0.0, -np.inf), F32)


def _head_pair_ones():
    return jnp.asarray(np.kron(np.eye(V7X_LANES // DIFF_DV), np.ones((DIFF_DV, DIFF_DV))), BF16)


def _attn_prompt_call(q, k, v, lam_rows, norm, layer, batch, seq, lam_init):
    tile = min(ATTN_TILE, seq // 2)
    nq = seq // tile
    nkb = nq // 2
    assert seq == nkb * 2 * tile, (seq, tile)
    steps = []
    for a in range(nq):
        steps += [(a, kb, _BOTH_FULL if kb else _BOTH_FULL_FIRST) for kb in range(a // 2)]
        steps.append((a, a // 2, _FULL_DIAG if a % 2 else _DIAG_ONLY))
    qi, kb, kind = (jnp.asarray([s[j] for s in steps], jnp.int32) for j in range(3))
    qspec = pl.BlockSpec((tile, DIFF_WIDTH), lambda b, p, qi, kb, kind: (b * nq + qi[p], 0))
    kspec = pl.BlockSpec((None, None, DIFF_WIDTH, 2 * tile), lambda b, p, qi, kb, kind: (layer, b, 0, kb[p]))
    per_layer = lambda shape: pl.BlockSpec((None,) + shape, lambda b, p, qi, kb, kind: (layer, 0, 0))
    const = lambda a: pl.BlockSpec(a.shape, lambda b, p, qi, kb, kind: (0,) * a.ndim)
    bias, ones = _chunk_causal_bias(tile), _head_pair_ones()
    est = (4 * tile * DIFF_WIDTH * (2 + 4 + 4 + 2) + N_MAPS * tile * (DIFF_WIDTH * 2 + 2 * V7X_LANES * 4)
           + 5 * N_MAPS * tile * tile * 4 + 2 * tile * tile * 4)
    return pl.pallas_call(
        functools.partial(_attn_prompt_kernel, tile=tile, lam_init=lam_init),
        grid_spec=pltpu.PrefetchScalarGridSpec(
            num_scalar_prefetch=3,
            grid=(batch, len(steps)),
            in_specs=[qspec, kspec, kspec, per_layer((V7X_SUBLANES, V7X_LANES)), per_layer((1, V7X_LANES)),
                      const(bias), const(ones)],
            out_specs=qspec,
            scratch_shapes=[pltpu.VMEM((N_MAPS * tile, DIFF_WIDTH), BF16),
                            pltpu.VMEM((N_MAPS, tile, V7X_LANES), F32),
                            pltpu.VMEM((N_MAPS, tile, V7X_LANES), F32)]),
        out_shape=jax.ShapeDtypeStruct((batch * seq, DIFF_WIDTH), BF16),
        compiler_params=pltpu.CompilerParams(
            dimension_semantics=("arbitrary", "arbitrary"), vmem_limit_bytes=_vmem_limit(est)),
        name="attn_prompt",
    )(qi, kb, kind, q, k, v, lam_rows, norm, bias, ones)


def _attn_sample_kernel(q_ref, k_ref, v_ref, kp_ref, vp_ref, lam_ref, norm_ref, ones_ref, o_ref, *, lam_init):
    qm_all = _masked_queries(q_ref[...])
    m, acc = _attend(qm_all, kp_ref[...].astype(BF16), _values_with_ones_t(vp_ref[...]), None, None, None,
                     transposed=True)
    m, acc = _attend(qm_all, k_ref[...].astype(BF16), _values_with_ones(v_ref[...]), m, acc, None,
                     transposed=False)
    o_ref[...] = _diff_finalize(acc, lam_ref, norm_ref, ones_ref, lam_init).astype(BF16)


def _attn_sample_call(q, k, v, k_past, v_past, lam_rows, norm, layer, batch, seq, lam_init):
    past = k_past.shape[3]
    qspec = pl.BlockSpec((seq, DIFF_WIDTH), lambda b: (b, 0))
    new = pl.BlockSpec((None, seq, DIFF_WIDTH), lambda b: (layer, b, 0))
    old = pl.BlockSpec((None, None, DIFF_WIDTH, past), lambda b: (layer, b, 0, 0))
    per_layer = lambda shape: pl.BlockSpec((None,) + shape, lambda b: (layer, 0, 0))
    ones = _head_pair_ones()
    est = 4 * past * DIFF_WIDTH * 4 + 3 * past * DIFF_WIDTH * 2 + 3 * N_MAPS * seq * past * 4
    return pl.pallas_call(
        functools.partial(_attn_sample_kernel, lam_init=lam_init),
        grid=(batch,),
        in_specs=[qspec, new, new, old, old, per_layer((V7X_SUBLANES, V7X_LANES)), per_layer((1, V7X_LANES)),
                  pl.BlockSpec(ones.shape, lambda b: (0, 0))],
        out_specs=qspec,
        out_shape=jax.ShapeDtypeStruct((batch * seq, DIFF_WIDTH), BF16),
        compiler_params=pltpu.CompilerParams(
            dimension_semantics=("arbitrary",), vmem_limit_bytes=_vmem_limit(est)),
        name="attn_sample",
    )(q, k, v, k_past, v_past, lam_rows, norm, ones)


def _pad_lanes(a, width):
    return jnp.pad(a, [(0, 0)] * (a.ndim - 1) + [(0, width - a.shape[-1])])


def _prepare_params(p):
    cols, start = [], 0
    for s in IN_SPLITS:
        cols.append(p["w_in"][..., start:start + s])
        start += s
    z, xbc, dt, gq, gk, gv, gr, ga, aq, ak, av = cols
    zeros = jnp.zeros(p["w_in"].shape[:2] + (_PAD_SMALL,), p["w_in"].dtype)
    w_in = jnp.concatenate([z, xbc, gq, gk, gv, gr, aq, dt, ga, zeros, ak, av], axis=-1).astype(BF16)
    w_kv_t = jnp.swapaxes(jnp.concatenate([ak, av], axis=-1), -1, -2).astype(BF16)
    row = lambda a: a[:, None, :]
    w_gate = jnp.zeros((DEPTH, V7X_LANES, V7X_LANES), F32)
    w_gate = w_gate.at[:, SSD_HEADS:SSD_HEADS + GLA_LOWRANK, :].set(p["gla_w_gate"]).astype(BF16)
    lam_rows = jnp.stack([p["diff_lambda_q1"], p["diff_lambda_k1"], p["diff_lambda_q2"], p["diff_lambda_k2"]], axis=1)
    lam_rows = jnp.pad(lam_rows, ((0, 0), (0, V7X_SUBLANES - 4), (0, V7X_LANES - DIFF_DK)))
    return dict(
        ffn1_norm=row(p["ffn1_norm"]), ffn2_norm=row(p["ffn2_norm"]), mix_norm=row(p["mix_norm"]),
        final_norm=p["final_norm"][None, :],
        ffn1=(p["ffn1_w_gate"].astype(BF16), p["ffn1_w_up"].astype(BF16), p["ffn1_w_down"].astype(BF16)),
        ffn2=(p["ffn2_w_gate"].astype(BF16), p["ffn2_w_up"].astype(BF16), p["ffn2_w_down"].astype(BF16)),
        w_in=w_in, w_kv_t=w_kv_t, w_out=p["w_out"].astype(BF16),
        conv_w=jnp.pad(p["ssd_conv_w"], ((0, 0), (0, V7X_SUBLANES - SSD_CONV), (0, 0))),
        conv_b=row(p["ssd_conv_b"]),
        dt_bias=row(_pad_lanes(p["ssd_dt_bias"], V7X_LANES)), a_log=row(_pad_lanes(p["ssd_a_log"], V7X_LANES)),
        d_exp=row(jnp.repeat(p["ssd_d"], SSD_HEAD_DIM, axis=-1)), ssd_norm=row(p["ssd_norm"]),
        gla_w_gate=w_gate, gla_b_gate=row(p["gla_b_gate"]), gla_norm=row(jnp.tile(p["gla_norm"], (1, GLA_HEADS))),
        lam_rows=lam_rows, diff_norm=row(jnp.tile(p["diff_norm"], (1, V7X_LANES // DIFF_DV))),
    )


def _gla_state_to_blocks(s):
    b = s.shape[0]
    st = jnp.swapaxes(s, -1, -2)
    eye = jnp.eye(GLA_HEADS, dtype=s.dtype)
    bd = st[:, :, :, None, :] * eye[None, :, None, :, None]
    return bd.reshape(b, GLA_HEADS * GLA_DV, GLA_HEADS * GLA_DK)


def _gla_state_from_blocks(bd):
    b = bd.shape[0]
    x = bd.reshape(b, GLA_HEADS, GLA_DV, GLA_HEADS, GLA_DK)
    st = jnp.stack([x[:, h, :, h, :] for h in range(GLA_HEADS)], axis=1)
    return jnp.swapaxes(st, -1, -2)


def _trunk(x, conv_st, ssd_st, gla_st, k_cache, v_cache, pp):
    batch, seq, d = x.shape
    prompt = k_cache is None
    x = x.reshape(batch * seq, d)
    convs, ssds, glas = [], [], []
    kv_shape = (DEPTH, batch, DIFF_WIDTH, seq) if prompt else (DEPTH, batch * seq, DIFF_WIDTH)
    kv_bufs = (jnp.zeros(kv_shape, F32), jnp.zeros(kv_shape, F32))
    hdr = V7X_SUBLANES
    for l in range(DEPTH):
        lam_init = 0.8 - 0.6 * float(np.exp(-0.3 * l))
        x = _ffn_call(x, l, pp["ffn1_norm"], *pp["ffn1"])
        z, xbc, gqk, gv, gr, aq, dtga, ak, av = _inproj_call(
            x, l, pp["mix_norm"], pp["w_in"], pp["w_kv_t"], kv_bufs, batch, seq, kv_transposed=prompt)
        kv_bufs = (ak, av)
        if prompt:
            conv_prev = jnp.zeros((batch, hdr, SSD_CONV_CH), F32)
            ssd0 = jnp.zeros((batch, SSD_HEADS, SSD_HEAD_DIM, SSD_STATE), F32)
            gla0 = jnp.zeros((batch, GLA_WIDTH, V7X_LANES), F32)
        else:
            conv_prev = jnp.pad(conv_st[l], ((0, 0), (hdr - (SSD_CONV - 1), 0), (0, 0)))
            ssd0 = ssd_st[l].astype(F32)
            gla0 = _gla_state_to_blocks(gla_st[l].astype(F32))
        y_ssd, conv_out, ssd_out = _ssd_call(xbc, z, dtga, conv_prev, ssd0, pp, l, batch, seq)
        y_gla, gla_out = _gla_call(gqk, gv, gr, dtga, gla0, pp, l, batch, seq)
        if prompt:
            y_diff = _attn_prompt_call(aq, ak, av, pp["lam_rows"], pp["diff_norm"], l, batch, seq, lam_init)
        else:
            feature_major = lambda c: jnp.transpose(c, (0, 1, 3, 4, 2)).reshape(DEPTH, batch, DIFF_WIDTH, c.shape[2])
            y_diff = _attn_sample_call(aq, ak, av, feature_major(k_cache), feature_major(v_cache),
                                       pp["lam_rows"], pp["diff_norm"], l, batch, seq, lam_init)
        last = l == DEPTH - 1
        x = _ffn_call(x, l, pp["ffn2_norm"], *pp["ffn2"], mix=(y_ssd, y_gla, y_diff, pp["w_out"]),
                      final_norm=pp["final_norm"] if last else None)
        convs.append(conv_out[:, hdr - (SSD_CONV - 1):, :])
        ssds.append(ssd_out)
        glas.append(_gla_state_from_blocks(gla_out))
    if prompt:
        k_out, v_out = (jnp.transpose(a.reshape(DEPTH, batch, DIFF_HEADS, DIFF_DV, seq), (0, 1, 4, 2, 3))
                        for a in kv_bufs)
    else:
        k_out, v_out = (a.reshape(DEPTH, batch, seq, DIFF_HEADS, DIFF_DV) for a in kv_bufs)
    return (x.reshape(batch, seq, d), jnp.stack(convs), jnp.stack(ssds), jnp.stack(glas), k_out, v_out)


def kernel(x_prompt, x_sample, state_ssd_conv, state_ssd, state_gla, cache_diff_k, cache_diff_v, ffn1_norm, ffn1_w_gate, ffn1_w_up, ffn1_w_down, mix_norm, w_in, w_out, ssd_conv_w, ssd_conv_b, ssd_dt_bias, ssd_a_log, ssd_d, ssd_norm, gla_w_gate, gla_b_gate, gla_norm, diff_lambda_q1, diff_lambda_k1, diff_lambda_q2, diff_lambda_k2, diff_norm, ffn2_norm, ffn2_w_gate, ffn2_w_up, ffn2_w_down, final_norm):
    pp = _prepare_params(dict(
        ffn1_norm=ffn1_norm, ffn1_w_gate=ffn1_w_gate, ffn1_w_up=ffn1_w_up, ffn1_w_down=ffn1_w_down,
        mix_norm=mix_norm, w_in=w_in, w_out=w_out, ssd_conv_w=ssd_conv_w, ssd_conv_b=ssd_conv_b,
        ssd_dt_bias=ssd_dt_bias, ssd_a_log=ssd_a_log, ssd_d=ssd_d, ssd_norm=ssd_norm,
        gla_w_gate=gla_w_gate, gla_b_gate=gla_b_gate, gla_norm=gla_norm,
        diff_lambda_q1=diff_lambda_q1, diff_lambda_k1=diff_lambda_k1, diff_lambda_q2=diff_lambda_q2,
        diff_lambda_k2=diff_lambda_k2, diff_norm=diff_norm,
        ffn2_norm=ffn2_norm, ffn2_w_gate=ffn2_w_gate, ffn2_w_up=ffn2_w_up, ffn2_w_down=ffn2_w_down,
        final_norm=final_norm))
    y_p, conv_p, ssd_p, gla_p, k_p, v_p = _trunk(x_prompt, None, None, None, None, None, pp)
    y_s, conv_s, ssd_s, gla_s, k_s, v_s = _trunk(
        x_sample, state_ssd_conv, state_ssd, state_gla, cache_diff_k, cache_diff_v, pp)
    return (y_p, y_s, conv_p, conv_s, ssd_p, ssd_s, gla_p, gla_s, k_p, k_s, v_p, v_s)
```

```python
import functools

import jax
import jax.numpy as jnp
import numpy as np
from jax import lax
from jax.experimental import pallas as pl
from jax.experimental.pallas import tpu as pltpu

F32 = jnp.float32
BF16 = jnp.bfloat16

EPS = 1e-5
DEPTH = 4
D_MODEL = 1024
D_FF = 2816
CHUNK = 64

SSD_WIDTH = 512
SSD_HEADS = 8
SSD_HEAD_DIM = 64
SSD_GROUPS = 2
SSD_STATE = 64
SSD_CONV = 4
SSD_CONV_CH = SSD_WIDTH + 2 * SSD_GROUPS * SSD_STATE

GLA_WIDTH = 256
GLA_HEADS = 4
GLA_DK = 32
GLA_DV = 64
GLA_LOWRANK = 16
GLA_TAU = 16.0

DIFF_WIDTH = 256
DIFF_HEADS = 4
DIFF_DK = 32
DIFF_DV = 64

IN_SPLITS = (SSD_WIDTH, SSD_CONV_CH, SSD_HEADS,
             GLA_HEADS * GLA_DK, GLA_HEADS * GLA_DK, GLA_WIDTH, GLA_WIDTH, GLA_LOWRANK,
             DIFF_HEADS * 2 * DIFF_DK, DIFF_HEADS * 2 * DIFF_DK, DIFF_WIDTH)

V7X_LANES = 128
V7X_SUBLANES = 8
V7X_VMEM_BYTES = 64 * 1024 * 1024

_PAD_SMALL = V7X_LANES - SSD_HEADS - GLA_LOWRANK
IN_PADDED = sum(IN_SPLITS) + _PAD_SMALL
O_Z, O_XBC, O_GQ, O_GV, O_GR, O_AQ, O_DTGA, O_AK, O_AV = 0, 512, 1280, 1536, 1792, 2048, 2304, 2432, 2688

TM = 512
TM_INPROJ = 1024
SEQ_TILE = 1024
SCAN_CHUNK = 128
ATTN_TILE = 512

_NT = (((1,), (1,)), ((), ()))
_TN = (((0,), (0,)), ((), ()))


def _vmem_limit(nbytes):
    return int(min(max(nbytes, 16 * 1024 * 1024), V7X_VMEM_BYTES - 6 * 1024 * 1024))


def _rmsnorm(x, g):
    ms = jnp.mean(x * x, axis=-1, keepdims=True)
    return x * lax.rsqrt(ms + EPS) * g


def _silu(x):
    half = 0.5 * x
    return half + half * jnp.tanh(half)


def _split3(x):
    hi = x.astype(BF16)
    r = x - hi.astype(F32)
    mid = r.astype(BF16)
    lo = (r - mid.astype(F32)).astype(BF16)
    return hi, mid, lo


def _dot_exact_l(a, x):
    return sum(jnp.dot(a, p, preferred_element_type=F32) for p in _split3(x))


def _dot_exact_r(x, a):
    return sum(jnp.dot(p, a, preferred_element_type=F32) for p in _split3(x))


def _dot_exact_nt(a, x):
    return sum(lax.dot_general(a, p, _NT, preferred_element_type=F32) for p in _split3(x))


def _resident(block_shape, index_map):
    return pl.BlockSpec(block_shape, index_map, pipeline_mode=pl.Buffered(1))


V7X_MXU_DIM = 256
_FF_CUT = (D_FF // V7X_MXU_DIM + 1) // 2 * V7X_MXU_DIM
FF_BLOCKS = ((0, _FF_CUT), (_FF_CUT, D_FF))


def _ffn_kernel(*refs, with_mix, with_final):
    it = iter(refs)
    x_ref = next(it)
    if with_mix:
        y1_ref, y2_ref, y3_ref, wo_ref = next(it), next(it), next(it), next(it)
    g_ref, wg_ref, wu_ref, wd_ref = next(it), next(it), next(it), next(it)
    if with_final:
        fn_ref = next(it)
    o_ref = next(it)

    x = x_ref[...]
    if with_mix:
        y = jnp.concatenate([y1_ref[...], y2_ref[...], y3_ref[...]], axis=-1)
        x = x + jnp.dot(y, wo_ref[...], preferred_element_type=F32)
    h = _rmsnorm(x, g_ref[...]).astype(BF16)
    acc = None
    for lo, hi in FF_BLOCKS:
        gate = jnp.dot(h, wg_ref[:, lo:hi], preferred_element_type=F32)
        up = jnp.dot(h, wu_ref[:, lo:hi], preferred_element_type=F32)
        act = (_silu(gate) * up).astype(BF16)
        part = jnp.dot(act, wd_ref[lo:hi, :], preferred_element_type=F32)
        acc = part if acc is None else acc + part
    x = x + 0.5 * acc
    if with_final:
        x = _rmsnorm(x, fn_ref[...])
    o_ref[...] = x


def _ffn_call(x, layer, norm, wg, wu, wd, mix=None, final_norm=None):
    t, d = x.shape
    tm = min(TM, t)
    tile = lambda w: pl.BlockSpec((tm, w), lambda i: (i, 0))
    per_layer = lambda shape: _resident((None,) + shape, lambda i: (layer, 0, 0))
    args, specs = [x], [tile(d)]
    if mix is not None:
        y1, y2, y3, wo = mix
        args += [y1, y2, y3, wo]
        specs += [tile(y1.shape[1]), tile(y2.shape[1]), tile(y3.shape[1]), per_layer((d, d))]
    args += [norm, wg, wu, wd]
    specs += [per_layer((1, d)), per_layer((d, D_FF)), per_layer((d, D_FF)), per_layer((D_FF, d))]
    if final_norm is not None:
        args.append(final_norm)
        specs.append(_resident((1, d), lambda i: (0, 0)))
    weights = 2 * (3 * d * D_FF + d * d)
    tiles = 2 * (2 * tm * d * 4 + tm * d * 2)
    temps = tm * (3 * d * 4 + _FF_CUT * (4 + 4 + 2))
    return pl.pallas_call(
        functools.partial(_ffn_kernel, with_mix=mix is not None, with_final=final_norm is not None),
        grid=(t // tm,),
        in_specs=specs,
        out_specs=tile(d),
        out_shape=jax.ShapeDtypeStruct((t, d), F32),
        compiler_params=pltpu.CompilerParams(
            dimension_semantics=("arbitrary",), vmem_limit_bytes=_vmem_limit(weights + tiles + 2 * temps)),
        name="ffn",
    )(*args)


def _inproj_kernel(x_ref, g_ref, w_ref, wt_ref, k_in_ref, v_in_ref,
                   z_ref, xbc_ref, gqk_ref, gv_ref, gr_ref, aq_ref, dtga_ref, ak_ref, av_ref, *, kv_transposed):
    del k_in_ref, v_in_ref
    h = _rmsnorm(x_ref[...], g_ref[...]).astype(BF16)
    if kv_transposed:
        u = jnp.dot(h, w_ref[:, :O_AK], preferred_element_type=F32)
        kv_t = lax.dot_general(wt_ref[...], h, _NT, preferred_element_type=F32)
        ak_ref[...] = kv_t[:DIFF_WIDTH]
        av_ref[...] = kv_t[DIFF_WIDTH:]
    else:
        u = jnp.dot(h, w_ref[...], preferred_element_type=F32)
        ak_ref[...] = u[:, O_AK:O_AV]
        av_ref[...] = u[:, O_AV:IN_PADDED]
    z_ref[...] = u[:, O_Z:O_XBC].astype(BF16)
    xbc_ref[...] = u[:, O_XBC:O_GQ]
    gqk_ref[...] = u[:, O_GQ:O_GV]
    gv_ref[...] = u[:, O_GV:O_GR].astype(BF16)
    gr_ref[...] = u[:, O_GR:O_AQ].astype(BF16)
    aq_ref[...] = (u[:, O_AQ:O_DTGA] * (DIFF_DK ** -0.5 * np.log2(np.e))).astype(BF16)
    dtga_ref[...] = u[:, O_DTGA:O_AK]


def _inproj_call(x, layer, norm, w, w_kv_t, kv_bufs, batch, seq, kv_transposed):
    t, d = x.shape
    widths = [(O_XBC - O_Z, BF16), (O_GQ - O_XBC, F32), (O_GV - O_GQ, F32), (O_GR - O_GV, BF16),
              (O_AQ - O_GR, BF16), (O_DTGA - O_AQ, BF16), (O_AK - O_DTGA, F32)]
    tm = min(TM_INPROJ, t)
    tile = lambda w_: pl.BlockSpec((tm, w_), lambda i: (i, 0))
    per_layer = lambda shape: _resident((None,) + shape, lambda i: (layer, 0, 0))
    out_specs = [tile(w_) for w_, _ in widths]
    out_shape = [jax.ShapeDtypeStruct((t, w_), dt) for w_, dt in widths]
    if kv_transposed:
        nt = seq // tm
        out_specs += [pl.BlockSpec((None, None, DIFF_WIDTH, tm), lambda i: (layer, i // nt, 0, i % nt))] * 2
        out_shape += [jax.ShapeDtypeStruct((DEPTH, batch, DIFF_WIDTH, seq), F32)] * 2
    else:
        out_specs += [pl.BlockSpec((None, tm, DIFF_WIDTH), lambda i: (layer, i, 0))] * 2
        out_shape += [jax.ShapeDtypeStruct((DEPTH, t, DIFF_WIDTH), F32)] * 2
    args = [x, norm, w, w_kv_t, *kv_bufs]
    in_specs = [tile(d), per_layer((1, d)), per_layer((d, IN_PADDED)), per_layer((2 * DIFF_WIDTH, d)),
                pl.BlockSpec(memory_space=pl.ANY), pl.BlockSpec(memory_space=pl.ANY)]
    aliases = {len(args) - 2: len(widths), len(args) - 1: len(widths) + 1}
    est = 2 * d * (IN_PADDED + 2 * DIFF_WIDTH) + 2 * tm * d * 4 + 3 * tm * IN_PADDED * 4
    return pl.pallas_call(
        functools.partial(_inproj_kernel, kv_transposed=kv_transposed),
        grid=(t // tm,),
        in_specs=in_specs,
        out_specs=out_specs,
        out_shape=out_shape,
        input_output_aliases=aliases,
        compiler_params=pltpu.CompilerParams(
            dimension_semantics=("arbitrary",), vmem_limit_bytes=_vmem_limit(est)),
        name="inproj",
    )(*args)


def _ssd_kernel(xbc_ref, z_ref, dtga_ref, cprev_ref, s0_ref, cw_ref, cb_ref, dtb_ref, alog_ref, dexp_ref,
                norm_ref, ltri_ref, ident_ref, expand_ref, dup_ref, fold_ref,
                y_ref, cout_ref, sout_ref,
                xp_scr, conv_scr, a_scr, dt_scr, s_scr, *, tt, cs):
    t = pl.program_id(1)
    hdr = V7X_SUBLANES
    bi = lax.broadcasted_iota(jnp.int32, (V7X_LANES, V7X_LANES), 0) // SSD_HEAD_DIM
    bj = lax.broadcasted_iota(jnp.int32, (V7X_LANES, V7X_LANES), 1) // SSD_STATE
    blockdiag = bi == bj

    @pl.when(t == 0)
    def _():
        xp_scr[0:hdr, :] = cprev_ref[...]
        for pr in range(SSD_HEADS // 2):
            both = jnp.concatenate([s0_ref[2 * pr], s0_ref[2 * pr + 1]], axis=0)
            s_scr[pr] = jnp.where(blockdiag, _dot_exact_r(both, dup_ref[...]), 0.0)

    xp_scr[hdr:hdr + tt, :] = xbc_ref[...]
    x3 = xp_scr[...].reshape((tt + hdr) // hdr, hdr, SSD_CONV_CH)
    sub = lax.broadcasted_iota(jnp.int32, (1, hdr, SSD_CONV_CH), 1)
    tap = lambda i: cw_ref[i:i + 1, :].reshape(1, 1, SSD_CONV_CH)
    conv = cb_ref[...].reshape(1, 1, SSD_CONV_CH) + x3[1:] * tap(SSD_CONV - 1)
    for d in range(1, SSD_CONV):
        rot = pltpu.roll(x3, d, axis=1)
        conv = conv + jnp.where(sub < d, rot[:-1], rot[1:]) * tap(SSD_CONV - 1 - d)
    conv_scr[...] = _silu(conv).reshape(tt, SSD_CONV_CH)
    xp_scr[0:hdr, :] = xp_scr[tt:tt + hdr, :]

    lane = lax.broadcasted_iota(jnp.int32, (tt, V7X_LANES), 1)
    x_dt = dtga_ref[...] + dtb_ref[...]
    dt = jnp.maximum(x_dt, 0.0) + jnp.log1p(jnp.exp(-jnp.abs(x_dt)))
    dt = jnp.where(lane < SSD_HEADS, dt, 0.0)
    dt_scr[...] = dt
    a_scr[...] = dt * (-jnp.exp(alog_ref[...]))

    left = lax.broadcasted_iota(jnp.int32, (cs, V7X_LANES), 1) < SSD_STATE
    ri = lax.broadcasted_iota(jnp.int32, (cs, cs), 0)
    ci = lax.broadcasted_iota(jnp.int32, (cs, cs), 1)
    causal = ci <= ri
    def chunk(c, carry):
        r0 = pl.multiple_of(c * cs, cs)
        conv_c = conv_scr[pl.ds(r0, cs), :]
        xs = conv_c[:, :SSD_WIDTH]
        bm = conv_c[:, SSD_WIDTH:SSD_WIDTH + V7X_LANES]
        cm = conv_c[:, SSD_WIDTH + V7X_LANES:]
        a_c = a_scr[pl.ds(r0, cs), :]
        dt_c = dt_scr[pl.ds(r0, cs), :]
        cum = _dot_exact_l(ltri_ref[...], a_c)
        cum_t = _dot_exact_nt(ident_ref[...], cum)
        last = cum[cs - 1:cs, :]
        e_cum = jnp.exp(cum)
        e_rev = jnp.exp(last - cum)
        small = jnp.concatenate([dt_c, e_cum, e_rev], axis=0).astype(BF16)
        wide = jnp.dot(small, expand_ref[...], preferred_element_type=F32)
        dt_x, ecum_x, erev_x = wide[0:cs], wide[cs:2 * cs], wide[2 * cs:3 * cs]
        elast_x = _dot_exact_r(jnp.broadcast_to(e_cum[cs - 1:cs, :], (V7X_SUBLANES, V7X_LANES)),
                               expand_ref[...])[0:1, :]
        xd = xs * dt_x
        bm_sw = pltpu.roll(bm, SSD_STATE, axis=1)
        cm_sw = pltpu.roll(cm, SSD_STATE, axis=1)
        ys = []
        for g in range(SSD_GROUPS):
            if g == 0:
                bdup, cdup, cmask = jnp.where(left, bm, bm_sw), jnp.where(left, cm, cm_sw), jnp.where(left, cm, 0.0)
            else:
                bdup, cdup, cmask = jnp.where(left, bm_sw, bm), jnp.where(left, cm_sw, cm), jnp.where(left, 0.0, cm)
            gmat = lax.dot_general(cmask.astype(BF16), bm.astype(BF16), _NT, preferred_element_type=F32)
            for pr in (2 * g, 2 * g + 1):
                sl = slice(V7X_LANES * pr, V7X_LANES * (pr + 1))
                xd_p = xd[:, sl]
                top = jnp.where(left, xd_p, 0.0).astype(BF16)
                bot = jnp.where(left, 0.0, xd_p).astype(BF16)
                scores = []
                for hd in range(2):
                    h = 2 * pr + hd
                    arg = cum[:, h:h + 1] - cum_t[h:h + 1, :]
                    decay = jnp.where(causal, jnp.exp(jnp.minimum(arg, 0.0)), 0.0)
                    scores.append((gmat * decay).astype(BF16))
                ce = (cdup * ecum_x[:, sl]).astype(BF16)
                s_pair = s_scr[pr]
                ys.append(jnp.dot(jnp.concatenate(scores, axis=1), jnp.concatenate([top, bot], axis=0),
                                  preferred_element_type=F32)
                          + lax.dot_general(ce, s_pair.astype(BF16), _NT, preferred_element_type=F32))
                bt = (bdup * erev_x[:, sl]).astype(BF16)
                upd = lax.dot_general(xd_p.astype(BF16), bt, _TN, preferred_element_type=F32)
                s_scr[pr] = s_pair * elast_x[:, sl] + jnp.where(blockdiag, upd, 0.0)
        y = jnp.concatenate(ys, axis=1) + dexp_ref[...] * xs
        y = y * _silu(z_ref[pl.ds(r0, cs), :].astype(F32))
        y_ref[pl.ds(r0, cs), :] = _rmsnorm(y, norm_ref[...]).astype(BF16)
        return carry

    lax.fori_loop(0, tt // cs, chunk, 0, unroll=True)

    @pl.when(t == pl.num_programs(1) - 1)
    def _():
        cout_ref[...] = xp_scr[0:hdr, :]
        for pr in range(SSD_HEADS // 2):
            both = _dot_exact_r(s_scr[pr], fold_ref[...])
            sout_ref[2 * pr] = both[:SSD_HEAD_DIM]
            sout_ref[2 * pr + 1] = both[SSD_HEAD_DIM:]


def _ssd_call(xbc, z, dtga, conv_prev, s0, consts, layer, batch, seq):
    tt = min(SEQ_TILE, seq)
    cs = min(SCAN_CHUNK, seq)
    nt = seq // tt
    hdr = V7X_SUBLANES
    tile = lambda w: pl.BlockSpec((tt, w), lambda b, t: (b * nt + t, 0))
    per_layer = lambda shape: _resident((None,) + shape, lambda b, t: (layer, 0, 0))
    const = lambda a: _resident(a.shape, lambda b, t: (0,) * a.ndim)
    npair = SSD_HEADS // 2
    ltri = jnp.asarray(np.tril(np.ones((cs, cs), np.float32)), BF16)
    ident = jnp.asarray(np.eye(V7X_LANES, dtype=np.float32), BF16)
    ex = np.zeros((V7X_LANES, SSD_WIDTH), np.float32)
    for h in range(SSD_HEADS):
        ex[h, h * SSD_HEAD_DIM:(h + 1) * SSD_HEAD_DIM] = 1.0
    expand = jnp.asarray(ex, BF16)
    eye = np.eye(SSD_STATE, dtype=np.float32)
    dup = jnp.asarray(np.concatenate([eye, eye], axis=1), BF16)
    fold = jnp.asarray(np.concatenate([eye, eye], axis=0), BF16)
    state_spec = pl.BlockSpec((None, SSD_HEADS, SSD_HEAD_DIM, SSD_STATE), lambda b, t: (b, 0, 0, 0))
    est = 4 * (tt * (SSD_CONV_CH * 4 + SSD_WIDTH * 2 + 128 * 4 + SSD_WIDTH * 2)
               + (tt + hdr) * SSD_CONV_CH * 4 + tt * SSD_CONV_CH * 4 + 16 * cs * cs * 4)
    return pl.pallas_call(
        functools.partial(_ssd_kernel, tt=tt, cs=cs),
        grid=(batch, nt),
        in_specs=[tile(SSD_CONV_CH), tile(SSD_WIDTH), tile(V7X_LANES),
                  pl.BlockSpec((None, hdr, SSD_CONV_CH), lambda b, t: (b, 0, 0)),
                  state_spec,
                  per_layer((hdr, SSD_CONV_CH)), per_layer((1, SSD_CONV_CH)), per_layer((1, V7X_LANES)),
                  per_layer((1, V7X_LANES)), per_layer((1, SSD_WIDTH)), per_layer((1, SSD_WIDTH)),
                  const(ltri), const(ident), const(expand), const(dup), const(fold)],
        out_specs=[tile(SSD_WIDTH),
                   pl.BlockSpec((None, hdr, SSD_CONV_CH), lambda b, t: (b, 0, 0)),
                   state_spec],
        out_shape=[jax.ShapeDtypeStruct((batch * seq, SSD_WIDTH), BF16),
                   jax.ShapeDtypeStruct((batch, hdr, SSD_CONV_CH), F32),
                   jax.ShapeDtypeStruct((batch, SSD_HEADS, SSD_HEAD_DIM, SSD_STATE), F32)],
        scratch_shapes=[pltpu.VMEM((tt + hdr, SSD_CONV_CH), F32), pltpu.VMEM((tt, SSD_CONV_CH), F32),
                        pltpu.VMEM((tt, V7X_LANES), F32), pltpu.VMEM((tt, V7X_LANES), F32),
                        pltpu.VMEM((npair, V7X_LANES, V7X_LANES), F32)],
        compiler_params=pltpu.CompilerParams(
            dimension_semantics=("arbitrary", "arbitrary"), vmem_limit_bytes=_vmem_limit(est)),
        name="ssd",
    )(xbc, z, dtga, conv_prev, s0, consts["conv_w"], consts["conv_b"], consts["dt_bias"], consts["a_log"],
      consts["d_exp"], consts["ssd_norm"], ltri, ident, expand, dup, fold)


def _gla_level_constants(cs):
    nl = int(np.log2(cs))
    idx = np.arange(cs)
    sums = np.zeros(((nl + 2) * cs, cs), np.float32)
    masks = np.zeros((nl + 1, cs, cs), np.float32)
    for lv in range(nl):
        s = 1 << lv
        blk = idx // (2 * s)
        right = (idx // s) % 2 == 1
        m = blk * 2 * s + s - 1
        for i in range(cs):
            if right[i]:
                sums[lv * cs + i, m[i] + 1:i + 1] = 1.0
            else:
                sums[lv * cs + i, i + 1:m[i] + 1] = 1.0
        masks[lv] = (blk[:, None] == blk[None, :]) & right[:, None] & ~right[None, :]
    masks[nl] = np.eye(cs)
    sums[nl * cs:(nl + 1) * cs] = np.tril(np.ones((cs, cs)))
    sums[(nl + 1) * cs:] = np.triu(np.ones((cs, cs)), 1)
    return nl, sums, masks


def _gla_kernel(gqk_ref, gv_ref, gr_ref, dtga_ref, s0_ref, wgate_ref, bgate_ref, norm_ref, sums_ref, masks_ref,
                ones_ref, y_ref, sout_ref, g_scr, st_scr, *, tt, cs, nl):
    t = pl.program_id(1)

    @pl.when(t == 0)
    def _():
        st_scr[...] = s0_ref[...]

    xg = jnp.dot(dtga_ref[...].astype(BF16), wgate_ref[...], preferred_element_type=F32) + bgate_ref[...]
    g_scr[...] = (jnp.minimum(xg, 0.0) - jnp.log1p(jnp.exp(-jnp.abs(xg)))) * (1.0 / GLA_TAU)

    qhead = lax.broadcasted_iota(jnp.int32, (cs, V7X_LANES), 1) // GLA_DK
    vhead = lax.broadcasted_iota(jnp.int32, (cs, GLA_WIDTH), 1) // GLA_DV
    bi = lax.broadcasted_iota(jnp.int32, (GLA_WIDTH, V7X_LANES), 0) // GLA_DV
    bj = lax.broadcasted_iota(jnp.int32, (GLA_WIDTH, V7X_LANES), 1) // GLA_DK
    blockdiag = bi == bj

    def chunk(c, carry):
        r0 = pl.multiple_of(c * cs, cs)
        g = g_scr[pl.ds(r0, cs), :]
        qk = gqk_ref[pl.ds(r0, cs), :]
        q = qk[:, :V7X_LANES] * (GLA_DK ** -0.5)
        k = qk[:, V7X_LANES:]
        v = gv_ref[pl.ds(r0, cs), :]
        g_hi = g.astype(BF16)
        g_lo = (g - g_hi.astype(F32)).astype(BF16)
        f2 = jnp.dot(sums_ref[...], jnp.concatenate([g_hi, g_lo], axis=1), preferred_element_type=F32)
        e = jnp.exp(f2[:, :V7X_LANES] + f2[:, V7X_LANES:])
        att = [None] * GLA_HEADS
        for lv in range(nl + 1):
            if lv < nl:
                e_lv = e[lv * cs:(lv + 1) * cs]
                qh, kh = q * e_lv, (k * e_lv).astype(BF16)
            else:
                qh, kh = q, k.astype(BF16)
            qstack = jnp.concatenate([jnp.where(qhead == h, qh, 0.0) for h in range(GLA_HEADS)], axis=0).astype(BF16)
            out = lax.dot_general(qstack, kh, _NT, preferred_element_type=F32)
            m = masks_ref[lv]
            for h in range(GLA_HEADS):
                part = m * out[h * cs:(h + 1) * cs]
                att[h] = part if att[h] is None else att[h] + part
        e_c = e[nl * cs:(nl + 1) * cs]
        e_rev = e[(nl + 1) * cs:(nl + 2) * cs]
        qt = (q * e_c).astype(BF16)
        kt = (k * e_rev).astype(BF16)
        a_cat = jnp.concatenate([a.astype(BF16) for a in att], axis=1)
        v_bd = jnp.concatenate([jnp.where(vhead == h, v, jnp.zeros_like(v)) for h in range(GLA_HEADS)], axis=0)
        st = st_scr[...]
        o = (jnp.dot(a_cat, v_bd, preferred_element_type=F32)
             + lax.dot_general(qt, st.astype(BF16), _NT, preferred_element_type=F32))
        upd = lax.dot_general(v, kt, _TN, preferred_element_type=F32)
        st_scr[...] = st * e_c[cs - 1:cs, :] + jnp.where(blockdiag, upd, 0.0)
        ms = _dot_exact_r(o * o, ones_ref[...]) * (1.0 / GLA_DV)
        r = gr_ref[pl.ds(r0, cs), :].astype(F32)
        y_ref[pl.ds(r0, cs), :] = (o * lax.rsqrt(ms + EPS) * norm_ref[...] * _silu(r)).astype(BF16)
        return carry

    lax.fori_loop(0, tt // cs, chunk, 0, unroll=True)

    @pl.when(t == pl.num_programs(1) - 1)
    def _():
        sout_ref[...] = st_scr[...]


def _gla_call(gqk, gv, gr, dtga, s0, consts, layer, batch, seq):
    tt = min(SEQ_TILE, seq)
    cs = min(SCAN_CHUNK, seq)
    nt = seq // tt
    nl, sums_np, masks_np = _gla_level_constants(cs)
    sums = jnp.asarray(sums_np, BF16)
    masks = jnp.asarray(masks_np, F32)
    ones = jnp.asarray(np.kron(np.eye(GLA_HEADS), np.ones((GLA_DV, GLA_DV))).astype(np.float32), BF16)
    tile = lambda w: pl.BlockSpec((tt, w), lambda b, t: (b * nt + t, 0))
    per_layer = lambda shape: _resident((None,) + shape, lambda b, t: (layer, 0, 0))
    const = lambda a: _resident(a.shape, lambda b, t: (0,) * a.ndim)
    est = 4 * (tt * (256 * 4 + 256 * 2 + 256 * 2 + 128 * 4 + 256 * 2)
               + 3 * (nl + 2) * cs * 128 * 4 + 12 * cs * cs * 4) + sums.size * 2 + masks.size * 4
    return pl.pallas_call(
        functools.partial(_gla_kernel, tt=tt, cs=cs, nl=nl),
        grid=(batch, nt),
        in_specs=[tile(2 * V7X_LANES), tile(GLA_WIDTH), tile(GLA_WIDTH), tile(V7X_LANES),
                  pl.BlockSpec((None, GLA_WIDTH, V7X_LANES), lambda b, t: (b, 0, 0)),
                  per_layer((V7X_LANES, V7X_LANES)), per_layer((1, V7X_LANES)), per_layer((1, GLA_WIDTH)),
                  const(sums), const(masks), const(ones)],
        out_specs=[tile(GLA_WIDTH), pl.BlockSpec((None, GLA_WIDTH, V7X_LANES), lambda b, t: (b, 0, 0))],
        out_shape=[jax.ShapeDtypeStruct((batch * seq, GLA_WIDTH), BF16),
                   jax.ShapeDtypeStruct((batch, GLA_WIDTH, V7X_LANES), F32)],
        scratch_shapes=[pltpu.VMEM((tt, V7X_LANES), F32), pltpu.VMEM((GLA_WIDTH, V7X_LANES), F32)],
        compiler_params=pltpu.CompilerParams(
            dimension_semantics=("arbitrary", "arbitrary"), vmem_limit_bytes=_vmem_limit(est)),
        name="gla",
    )(gqk, gv, gr, dtga, s0, consts["gla_w_gate"], consts["gla_b_gate"], consts["gla_norm"], sums, masks, ones)


N_MAPS = 2 * DIFF_HEADS


def _masked_queries(q):
    group = lax.broadcasted_iota(jnp.int32, q.shape, 1) // DIFF_DK
    return jnp.concatenate([jnp.where(group == hm, q, jnp.zeros_like(q)) for hm in range(N_MAPS)], axis=0)


def _values_with_ones(v):
    left = lax.broadcasted_iota(jnp.int32, (v.shape[0], V7X_LANES), 1) < DIFF_DV
    out = []
    for pr in range(DIFF_HEADS // 2):
        vp = v[:, V7X_LANES * pr:V7X_LANES * (pr + 1)]
        out.append(jnp.where(left, vp, 1.0).astype(BF16))
        out.append(jnp.where(left, 1.0, vp).astype(BF16))
    return out


def _values_with_ones_t(v_t):
    ones = jnp.ones((DIFF_DV, v_t.shape[1]), BF16)
    out = []
    for h in range(DIFF_HEADS):
        vh = v_t[h * DIFF_DV:(h + 1) * DIFF_DV, :].astype(BF16)
        out.append(jnp.concatenate([ones, vh] if h % 2 else [vh, ones], axis=0))
    return out


def _lane_tile(m, width):
    if width % V7X_LANES:
        return m[:, :width]
    return jnp.concatenate([m] * (width // V7X_LANES), axis=1)


def _attend(qm_all, k, vhs, m_prev, acc_prev, bias, transposed):
    tq = qm_all.shape[0] // N_MAPS
    if transposed:
        s_all = jnp.dot(qm_all, k, preferred_element_type=F32)
    else:
        s_all = lax.dot_general(qm_all, k, _NT, preferred_element_type=F32)
    tk = s_all.shape[1]
    m_out, acc_out = [], []
    for h in range(DIFF_HEADS):
        ps, alphas = [], []
        for mp in range(2):
            hm = 2 * h + mp
            s = s_all[hm * tq:(hm + 1) * tq]
            if bias is not None:
                s = s + bias
            m_new = jnp.max(s, axis=-1, keepdims=True)
            if m_prev is None:
                m_new = jnp.broadcast_to(m_new, (tq, V7X_LANES))
            else:
                m_new = jnp.maximum(m_prev[hm], m_new)
                alphas.append(jnp.exp2(m_prev[hm] - m_new))
            m_out.append(m_new)
            ps.append(jnp.exp2(s - _lane_tile(m_new, tk)).astype(BF16))
        p_both = jnp.concatenate(ps, axis=0)
        if transposed:
            pv = lax.dot_general(p_both, vhs[h], _NT, preferred_element_type=F32)
        else:
            pv = jnp.dot(p_both, vhs[h], preferred_element_type=F32)
        for mp in range(2):
            new = pv[mp * tq:(mp + 1) * tq]
            acc_out.append(new if m_prev is None else alphas[mp] * acc_prev[2 * h + mp] + new)
    return m_out, acc_out


def _diff_finalize(accs, lam_ref, norm_ref, ones_ref, lam_init):
    lq = lam_ref[...]
    s1 = jnp.sum(lq[0:1, :] * lq[1:2, :], axis=-1, keepdims=True)
    s2 = jnp.sum(lq[2:3, :] * lq[3:4, :], axis=-1, keepdims=True)
    lam = jnp.exp(s1) - jnp.exp(s2) + lam_init
    tq = accs[0].shape[0]
    left = lax.broadcasted_iota(jnp.int32, (tq, V7X_LANES), 1) < DIFF_DV
    outs = []
    for pr in range(DIFF_HEADS // 2):
        ratio = []
        for mp in range(2):
            even, odd = accs[4 * pr + mp], accs[4 * pr + 2 + mp]
            num = jnp.where(left, even, odd)
            den = pltpu.roll(jnp.where(left, odd, even), DIFF_DV, axis=1)
            ratio.append(num / den)
        o = ratio[0] - lam * ratio[1]
        ms = _dot_exact_r(o * o, ones_ref[...]) * (1.0 / DIFF_DV)
        outs.append(o * lax.rsqrt(ms + EPS) * norm_ref[...] * (1.0 - lam_init))
    return jnp.concatenate(outs, axis=1)


_BOTH_FULL = 0
_FULL_DIAG = 1
_DIAG_ONLY = 2
_BOTH_FULL_FIRST = 3


def _attn_prompt_kernel(qi_ref, kb_ref, kind_ref, q_ref, k_ref, v_ref, lam_ref, norm_ref, bias_ref, ones_ref, o_ref,
                        qm_scr, m_scr, acc_scr, *, tile, lam_init):
    p = pl.program_id(1)
    kind = kind_ref[p]

    @pl.when(kb_ref[p] == 0)
    def _():
        qm_scr[...] = _masked_queries(q_ref[...])

    @pl.when((kb_ref[p] == 0) & (kind != _BOTH_FULL_FIRST))
    def _():
        m_scr[...] = jnp.full(m_scr.shape, -jnp.inf, F32)
        acc_scr[...] = jnp.zeros(acc_scr.shape, F32)

    def run(diagonals, final, first=False):
        m = None if first else [m_scr[hm] for hm in range(N_MAPS)]
        acc = None if first else [acc_scr[hm] for hm in range(N_MAPS)]

        def attend(q_rows, m, acc, lo, hi, bias):
            return _attend(q_rows, k_ref[:, lo:hi].astype(BF16), _values_with_ones_t(v_ref[:, lo:hi]), m, acc, bias,
                           transposed=True)

        for half, diagonal in enumerate(diagonals):
            m, acc = attend(qm_scr[...], m, acc, half * tile, (half + 1) * tile, bias_ref[...] if diagonal else None)
        if final:
            o_ref[...] = _diff_finalize(acc, lam_ref, norm_ref, ones_ref, lam_init).astype(BF16)
        else:
            for hm in range(N_MAPS):
                m_scr[hm] = m[hm]
                acc_scr[hm] = acc[hm]

    @pl.when(kind == _BOTH_FULL)
    def _():
        run([False, False], final=False)

    @pl.when(kind == _BOTH_FULL_FIRST)
    def _():
        run([False, False], final=False, first=True)

    @pl.when(kind == _FULL_DIAG)
    def _():
        run([False, True], final=True)

    @pl.when(kind == _DIAG_ONLY)
    def _():
        run([True], final=True)


def _chunk_causal_bias(tile):
    c = np.arange(tile) // CHUNK
    return jnp.asarray(np.where(c[None, :] <= c[:, None], 0.0, -np.inf), F32)


def _head_pair_ones():
    return jnp.asarray(np.kron(np.eye(V7X_LANES // DIFF_DV), np.ones((DIFF_DV, DIFF_DV))), BF16)


def _attn_prompt_call(q, k, v, lam_rows, norm, layer, batch, seq, lam_init):
    tile = min(ATTN_TILE, seq // 2)
    nq = seq // tile
    nkb = nq // 2
    assert seq == nkb * 2 * tile, (seq, tile)
    steps = []
    for a in range(nq):
        steps += [(a, kb, _BOTH_FULL if kb else _BOTH_FULL_FIRST) for kb in range(a // 2)]
        steps.append((a, a // 2, _FULL_DIAG if a % 2 else _DIAG_ONLY))
    qi, kb, kind = (jnp.asarray([s[j] for s in steps], jnp.int32) for j in range(3))
    qspec = pl.BlockSpec((tile, DIFF_WIDTH), lambda b, p, qi, kb, kind: (b * nq + qi[p], 0))
    kspec = pl.BlockSpec((None, None, DIFF_WIDTH, 2 * tile), lambda b, p, qi, kb, kind: (layer, b, 0, kb[p]))
    per_layer = lambda shape: pl.BlockSpec((None,) + shape, lambda b, p, qi, kb, kind: (layer, 0, 0))
    const = lambda a: pl.BlockSpec(a.shape, lambda b, p, qi, kb, kind: (0,) * a.ndim)
    bias, ones = _chunk_causal_bias(tile), _head_pair_ones()
    est = (4 * tile * DIFF_WIDTH * (2 + 4 + 4 + 2) + N_MAPS * tile * (DIFF_WIDTH * 2 + 2 * V7X_LANES * 4)
           + 5 * N_MAPS * tile * tile * 4 + 2 * tile * tile * 4)
    return pl.pallas_call(
        functools.partial(_attn_prompt_kernel, tile=tile, lam_init=lam_init),
        grid_spec=pltpu.PrefetchScalarGridSpec(
            num_scalar_prefetch=3,
            grid=(batch, len(steps)),
            in_specs=[qspec, kspec, kspec, per_layer((V7X_SUBLANES, V7X_LANES)), per_layer((1, V7X_LANES)),
                      const(bias), const(ones)],
            out_specs=qspec,
            scratch_shapes=[pltpu.VMEM((N_MAPS * tile, DIFF_WIDTH), BF16),
                            pltpu.VMEM((N_MAPS, tile, V7X_LANES), F32),
                            pltpu.VMEM((N_MAPS, tile, V7X_LANES), F32)]),
        out_shape=jax.ShapeDtypeStruct((batch * seq, DIFF_WIDTH), BF16),
        compiler_params=pltpu.CompilerParams(
            dimension_semantics=("arbitrary", "arbitrary"), vmem_limit_bytes=_vmem_limit(est)),
        name="attn_prompt",
    )(qi, kb, kind, q, k, v, lam_rows, norm, bias, ones)


def _attn_sample_kernel(q_ref, k_ref, v_ref, kp_ref, vp_ref, lam_ref, norm_ref, ones_ref, o_ref, *, lam_init):
    qm_all = _masked_queries(q_ref[...])
    m, acc = _attend(qm_all, kp_ref[...].astype(BF16), _values_with_ones_t(vp_ref[...]), None, None, None,
                     transposed=True)
    m, acc = _attend(qm_all, k_ref[...].astype(BF16), _values_with_ones(v_ref[...]), m, acc, None,
                     transposed=False)
    o_ref[...] = _diff_finalize(acc, lam_ref, norm_ref, ones_ref, lam_init).astype(BF16)


def _attn_sample_call(q, k, v, k_past, v_past, lam_rows, norm, layer, batch, seq, lam_init):
    past = k_past.shape[3]
    qspec = pl.BlockSpec((seq, DIFF_WIDTH), lambda b: (b, 0))
    new = pl.BlockSpec((None, seq, DIFF_WIDTH), lambda b: (layer, b, 0))
    old = pl.BlockSpec((None, None, DIFF_WIDTH, past), lambda b: (layer, b, 0, 0))
    per_layer = lambda shape: pl.BlockSpec((None,) + shape, lambda b: (layer, 0, 0))
    ones = _head_pair_ones()
    est = 4 * past * DIFF_WIDTH * 4 + 3 * past * DIFF_WIDTH * 2 + 3 * N_MAPS * seq * past * 4
    return pl.pallas_call(
        functools.partial(_attn_sample_kernel, lam_init=lam_init),
        grid=(batch,),
        in_specs=[qspec, new, new, old, old, per_layer((V7X_SUBLANES, V7X_LANES)), per_layer((1, V7X_LANES)),
                  pl.BlockSpec(ones.shape, lambda b: (0, 0))],
        out_specs=qspec,
        out_shape=jax.ShapeDtypeStruct((batch * seq, DIFF_WIDTH), BF16),
        compiler_params=pltpu.CompilerParams(
            dimension_semantics=("arbitrary",), vmem_limit_bytes=_vmem_limit(est)),
        name="attn_sample",
    )(q, k, v, k_past, v_past, lam_rows, norm, ones)


def _pad_lanes(a, width):
    return jnp.pad(a, [(0, 0)] * (a.ndim - 1) + [(0, width - a.shape[-1])])


def _prepare_params(p):
    cols, start = [], 0
    for s in IN_SPLITS:
        cols.append(p["w_in"][..., start:start + s])
        start += s
    z, xbc, dt, gq, gk, gv, gr, ga, aq, ak, av = cols
    zeros = jnp.zeros(p["w_in"].shape[:2] + (_PAD_SMALL,), p["w_in"].dtype)
    w_in = jnp.concatenate([z, xbc, gq, gk, gv, gr, aq, dt, ga, zeros, ak, av], axis=-1).astype(BF16)
    w_kv_t = jnp.swapaxes(jnp.concatenate([ak, av], axis=-1), -1, -2).astype(BF16)
    row = lambda a: a[:, None, :]
    w_gate = jnp.zeros((DEPTH, V7X_LANES, V7X_LANES), F32)
    w_gate = w_gate.at[:, SSD_HEADS:SSD_HEADS + GLA_LOWRANK, :].set(p["gla_w_gate"]).astype(BF16)
    lam_rows = jnp.stack([p["diff_lambda_q1"], p["diff_lambda_k1"], p["diff_lambda_q2"], p["diff_lambda_k2"]], axis=1)
    lam_rows = jnp.pad(lam_rows, ((0, 0), (0, V7X_SUBLANES - 4), (0, V7X_LANES - DIFF_DK)))
    return dict(
        ffn1_norm=row(p["ffn1_norm"]), ffn2_norm=row(p["ffn2_norm"]), mix_norm=row(p["mix_norm"]),
        final_norm=p["final_norm"][None, :],
        ffn1=(p["ffn1_w_gate"].astype(BF16), p["ffn1_w_up"].astype(BF16), p["ffn1_w_down"].astype(BF16)),
        ffn2=(p["ffn2_w_gate"].astype(BF16), p["ffn2_w_up"].astype(BF16), p["ffn2_w_down"].astype(BF16)),
        w_in=w_in, w_kv_t=w_kv_t, w_out=p["w_out"].astype(BF16),
        conv_w=jnp.pad(p["ssd_conv_w"], ((0, 0), (0, V7X_SUBLANES - SSD_CONV), (0, 0))),
        conv_b=row(p["ssd_conv_b"]),
        dt_bias=row(_pad_lanes(p["ssd_dt_bias"], V7X_LANES)), a_log=row(_pad_lanes(p["ssd_a_log"], V7X_LANES)),
        d_exp=row(jnp.repeat(p["ssd_d"], SSD_HEAD_DIM, axis=-1)), ssd_norm=row(p["ssd_norm"]),
        gla_w_gate=w_gate, gla_b_gate=row(p["gla_b_gate"]), gla_norm=row(jnp.tile(p["gla_norm"], (1, GLA_HEADS))),
        lam_rows=lam_rows, diff_norm=row(jnp.tile(p["diff_norm"], (1, V7X_LANES // DIFF_DV))),
    )


def _gla_state_to_blocks(s):
    b = s.shape[0]
    st = jnp.swapaxes(s, -1, -2)
    eye = jnp.eye(GLA_HEADS, dtype=s.dtype)
    bd = st[:, :, :, None, :] * eye[None, :, None, :, None]
    return bd.reshape(b, GLA_HEADS * GLA_DV, GLA_HEADS * GLA_DK)


def _gla_state_from_blocks(bd):
    b = bd.shape[0]
    x = bd.reshape(b, GLA_HEADS, GLA_DV, GLA_HEADS, GLA_DK)
    st = jnp.stack([x[:, h, :, h, :] for h in range(GLA_HEADS)], axis=1)
    return jnp.swapaxes(st, -1, -2)


def _trunk(x, conv_st, ssd_st, gla_st, k_cache, v_cache, pp):
    batch, seq, d = x.shape
    prompt = k_cache is None
    x = x.reshape(batch * seq, d)
    convs, ssds, glas = [], [], []
    kv_shape = (DEPTH, batch, DIFF_WIDTH, seq) if prompt else (DEPTH, batch * seq, DIFF_WIDTH)
    kv_bufs = (jnp.zeros(kv_shape, F32), jnp.zeros(kv_shape, F32))
    hdr = V7X_SUBLANES
    for l in range(DEPTH):
        lam_init = 0.8 - 0.6 * float(np.exp(-0.3 * l))
        x = _ffn_call(x, l, pp["ffn1_norm"], *pp["ffn1"])
        z, xbc, gqk, gv, gr, aq, dtga, ak, av = _inproj_call(
            x, l, pp["mix_norm"], pp["w_in"], pp["w_kv_t"], kv_bufs, batch, seq, kv_transposed=prompt)
        kv_bufs = (ak, av)
        if prompt:
            conv_prev = jnp.zeros((batch, hdr, SSD_CONV_CH), F32)
            ssd0 = jnp.zeros((batch, SSD_HEADS, SSD_HEAD_DIM, SSD_STATE), F32)
            gla0 = jnp.zeros((batch, GLA_WIDTH, V7X_LANES), F32)
        else:
            conv_prev = jnp.pad(conv_st[l], ((0, 0), (hdr - (SSD_CONV - 1), 0), (0, 0)))
            ssd0 = ssd_st[l].astype(F32)
            gla0 = _gla_state_to_blocks(gla_st[l].astype(F32))
        y_ssd, conv_out, ssd_out = _ssd_call(xbc, z, dtga, conv_prev, ssd0, pp, l, batch, seq)
        y_gla, gla_out = _gla_call(gqk, gv, gr, dtga, gla0, pp, l, batch, seq)
        if prompt:
            y_diff = _attn_prompt_call(aq, ak, av, pp["lam_rows"], pp["diff_norm"], l, batch, seq, lam_init)
        else:
            feature_major = lambda c: jnp.transpose(c, (0, 1, 3, 4, 2)).reshape(DEPTH, batch, DIFF_WIDTH, c.shape[2])
            y_diff = _attn_sample_call(aq, ak, av, feature_major(k_cache), feature_major(v_cache),
                                       pp["lam_rows"], pp["diff_norm"], l, batch, seq, lam_init)
        last = l == DEPTH - 1
        x = _ffn_call(x, l, pp["ffn2_norm"], *pp["ffn2"], mix=(y_ssd, y_gla, y_diff, pp["w_out"]),
                      final_norm=pp["final_norm"] if last else None)
        convs.append(conv_out[:, hdr - (SSD_CONV - 1):, :])
        ssds.append(ssd_out)
        glas.append(_gla_state_from_blocks(gla_out))
    if prompt:
        k_out, v_out = (jnp.transpose(a.reshape(DEPTH, batch, DIFF_HEADS, DIFF_DV, seq), (0, 1, 4, 2, 3))
                        for a in kv_bufs)
    else:
        k_out, v_out = (a.reshape(DEPTH, batch, seq, DIFF_HEADS, DIFF_DV) for a in kv_bufs)
    return (x.reshape(batch, seq, d), jnp.stack(convs), jnp.stack(ssds), jnp.stack(glas), k_out, v_out)


def kernel(x_prompt, x_sample, state_ssd_conv, state_ssd, state_gla, cache_diff_k, cache_diff_v, ffn1_norm, ffn1_w_gate, ffn1_w_up, ffn1_w_down, mix_norm, w_in, w_out, ssd_conv_w, ssd_conv_b, ssd_dt_bias, ssd_a_log, ssd_d, ssd_norm, gla_w_gate, gla_b_gate, gla_norm, diff_lambda_q1, diff_lambda_k1, diff_lambda_q2, diff_lambda_k2, diff_norm, ffn2_norm, ffn2_w_gate, ffn2_w_up, ffn2_w_down, final_norm):
    pp = _prepare_params(dict(
        ffn1_norm=ffn1_norm, ffn1_w_gate=ffn1_w_gate, ffn1_w_up=ffn1_w_up, ffn1_w_down=ffn1_w_down,
        mix_norm=mix_norm, w_in=w_in, w_out=w_out, ssd_conv_w=ssd_conv_w, ssd_conv_b=ssd_conv_b,
        ssd_dt_bias=ssd_dt_bias, ssd_a_log=ssd_a_log, ssd_d=ssd_d, ssd_norm=ssd_norm,
        gla_w_gate=gla_w_gate, gla_b_gate=gla_b_gate, gla_norm=gla_norm,
        diff_lambda_q1=diff_lambda_q1, diff_lambda_k1=diff_lambda_k1, diff_lambda_q2=diff_lambda_q2,
        diff_lambda_k2=diff_lambda_k2, diff_norm=diff_norm,
        ffn2_norm=ffn2_norm, ffn2_w_gate=ffn2_w_gate, ffn2_w_up=ffn2_w_up, ffn2_w_down=ffn2_w_down,
        final_norm=final_norm))
    y_p, conv_p, ssd_p, gla_p, k_p, v_p = _trunk(x_prompt, None, None, None, None, None, pp)
    y_s, conv_s, ssd_s, gla_s, k_s, v_s = _trunk(
        x_sample, state_ssd_conv, state_ssd, state_gla, cache_diff_k, cache_diff_v, pp)
    return (y_p, y_s, conv_p, conv_s, ssd_p, ssd_s, gla_p, gla_s, k_p, k_s, v_p, v_s)
```
